```python
import jax
import jax.numpy as jnp
from jax import lax
import numpy as np

D_MODEL = 2048
BATCH = 4
SEQ = 4096
DEPTH = 1

CHUNK = 64
SUB_CHUNK = 16
CONV_WIDTH = 4
GDN_HEAD_DIM = 128
GDN_HEADS = D_MODEL // (2 * GDN_HEAD_DIM)
GDN_WIDTH = GDN_HEADS * GDN_HEAD_DIM
HGRN_HEAD_DIM = 128
HGRN_VALUE_DIM = 128
HGRN_HEADS = D_MODEL // (2 * HGRN_VALUE_DIM)
HGRN_WIDTH = HGRN_HEADS * HGRN_HEAD_DIM
HGRN_V_WIDTH = HGRN_HEADS * HGRN_VALUE_DIM
D_MIX = GDN_WIDTH + HGRN_V_WIDTH
IN_PROJ_WIDTH = 4 * GDN_WIDTH + 2 * GDN_HEADS + 2 * HGRN_WIDTH + 2 * HGRN_V_WIDTH
D_FF = 4 * D_MODEL
NORM_EPS = 1e-6
L2_EPS = 1e-6

kernel_name = 'hybrid_gdn_hgrn2_block'


def rms_norm(x, w):
    xf = x.astype(jnp.float32)
    y = xf * lax.rsqrt(jnp.mean(xf * xf, axis=-1, keepdims=True) + NORM_EPS)
    return (y * w.astype(jnp.float32)).astype(x.dtype)


def head_rms_norm(o, w):
    return o * lax.rsqrt(jnp.mean(o * o, axis=-1, keepdims=True) + NORM_EPS) * w.astype(jnp.float32)


def l2_normalize(t):
    return t * lax.rsqrt(jnp.sum(t * t, axis=-1, keepdims=True) + L2_EPS)


def to_heads(t, n_heads):
    b, s, _ = t.shape
    return t.reshape(b, s, n_heads, -1).transpose(0, 2, 1, 3).astype(jnp.float32)


def from_heads(t):
    b, h, s, d = t.shape
    return t.transpose(0, 2, 1, 3).reshape(b, s, h * d)


def causal_depthwise_conv(t, w):
    return lax.conv_general_dilated(
        t, w[:, None, :].astype(t.dtype), window_strides=(1,),
        padding=[(w.shape[0] - 1, 0)], dimension_numbers=('NWC', 'WIO', 'NWC'),
        feature_group_count=t.shape[-1])


def gated_delta_rule_chunked(q, k, v, beta, g):
    b_, h_, t_, dk = q.shape
    dv = v.shape[-1]
    n_chunks = t_ // CHUNK
    q, k, v, beta, g = (t.reshape(b_, h_, n_chunks, CHUNK, *t.shape[3:]) for t in (q, k, v, beta, g))
    G = jnp.cumsum(g, axis=-1)
    pos = jnp.arange(CHUNK)
    incl = pos[:, None] >= pos[None, :]
    strict = pos[:, None] > pos[None, :]
    decay = jnp.exp(jnp.where(incl, G[..., :, None] - G[..., None, :], -jnp.inf))
    kk = jnp.einsum('bhncd,bhnsd->bhncs', k, k)
    unit_lower = jnp.where(strict, beta[..., :, None] * kk * decay, 0.0) + jnp.eye(CHUNK, dtype=jnp.float32)
    rhs = beta[..., None] * jnp.concatenate([v, jnp.exp(G)[..., None] * k], axis=-1)
    sol = lax.linalg.triangular_solve(unit_lower, rhs, left_side=True, lower=True, unit_diagonal=True)
    u_v, w = sol[..., :dv], sol[..., dv:]
    attn = jnp.einsum('bhncd,bhnsd->bhncs', q, k) * decay
    q_g = q * jnp.exp(G)[..., None]
    g_last = G[..., -1:]
    k_end = k * jnp.exp(g_last - G)[..., None]
    state_decay = jnp.exp(G[..., -1])

    def step(S, xs):
        u_v_c, w_c, q_c, attn_c, k_c, sd_c = xs
        u = u_v_c - jnp.einsum('bhcd,bhde->bhce', w_c, S)
        o = jnp.einsum('bhcd,bhde->bhce', q_c, S) + jnp.einsum('bhcs,bhse->bhce', attn_c, u)
        S = S * sd_c[..., None, None] + jnp.einsum('bhcd,bhce->bhde', k_c, u)
        return S, o

    xs = tuple(jnp.moveaxis(t, 2, 0) for t in (u_v, w, q_g, attn, k_end, state_decay))
    s0 = jnp.zeros((b_, h_, dk, dv), jnp.float32)
    _, o = lax.scan(step, s0, xs)
    return jnp.moveaxis(o, 0, 2).reshape(b_, h_, t_, dv)


def hgrn2_chunked(q, k, v, log_f):
    b_, h_, t_, dk = q.shape
    dv = v.shape[-1]
    n_chunks = t_ // CHUNK
    n_sub = CHUNK // SUB_CHUNK
    b_cum = jnp.cumsum(log_f.reshape(b_, h_, n_chunks, CHUNK, dk), axis=3)

    def blocks(t):
        return jnp.moveaxis(t.reshape(b_, h_, n_chunks, n_sub, SUB_CHUNK, t.shape[-1]), 2, 0)

    sub = jnp.arange(SUB_CHUNK)
    diag_mask = (sub[:, None] >= sub[None, :])[:, :, None]
    blk = jnp.arange(n_sub)
    off_mask = (blk[:, None] > blk[None, :])[:, :, None]

    def step(S, xs):
        qc, kc, vc, bc = xs
        b_end = bc[..., -1, :]
        b_start = jnp.concatenate([jnp.zeros_like(b_end[..., :1, :]), b_end[..., :-1, :]], axis=-2)
        o_inter = jnp.einsum('bhsid,bhde->bhsie', qc * jnp.exp(bc), S)
        d_diag = jnp.exp(jnp.where(diag_mask, bc[..., :, None, :] - bc[..., None, :, :], -jnp.inf))
        a_diag = jnp.einsum('bhsid,bhsijd,bhsjd->bhsij', qc, d_diag, kc)
        q_rel = qc * jnp.exp(bc - b_start[..., None, :])
        k_rel = kc * jnp.exp(b_end[..., None, :] - bc)
        d_off = jnp.exp(jnp.where(off_mask, b_start[..., :, None, :] - b_end[..., None, :, :], -jnp.inf))
        a_off = jnp.einsum('bhxid,bhxyd,bhyjd->bhxyij', q_rel, d_off, k_rel)
        o = (o_inter + jnp.einsum('bhsij,bhsje->bhsie', a_diag, vc)
             + jnp.einsum('bhxyij,bhyje->bhxie', a_off, vc))
        b_last = b_end[..., -1, :]
        k_state = kc * jnp.exp(b_last[:, :, None, None, :] - bc)
        S = S * jnp.exp(b_last)[..., None] + jnp.einsum('bhsjd,bhsje->bhde', k_state, vc)
        return S, o

    xs = (blocks(q), blocks(k), blocks(v), blocks(b_cum))
    s0 = jnp.zeros((b_, h_, dk, dv), jnp.float32)
    _, o = lax.scan(step, s0, xs)
    return jnp.moveaxis(o, 0, 2).reshape(b_, h_, t_, dv)


def gated_deltanet_group(qkv, z, a, b, conv_w, a_log, dt_bias, norm_w):
    qkv = jax.nn.silu(causal_depthwise_conv(qkv, conv_w))
    q, k, v = jnp.split(qkv, 3, axis=-1)
    q = l2_normalize(to_heads(q, GDN_HEADS)) * (GDN_HEAD_DIM ** -0.5)
    k = l2_normalize(to_heads(k, GDN_HEADS))
    v = to_heads(v, GDN_HEADS)
    beta = jax.nn.sigmoid(b.astype(jnp.float32)).transpose(0, 2, 1)
    g = (-jnp.exp(a_log.astype(jnp.float32))
         * jax.nn.softplus(a.astype(jnp.float32) + dt_bias.astype(jnp.float32))).transpose(0, 2, 1)
    o = gated_delta_rule_chunked(q, k, v, beta, g)
    o = head_rms_norm(o, norm_w) * jax.nn.silu(to_heads(z, GDN_HEADS))
    return from_heads(o)


def hgrn2_group(q, f, i, g, lower_bound, norm_w):
    lb = lower_bound.reshape(HGRN_HEADS, 1, HGRN_HEAD_DIM)
    f_logit = to_heads(f, HGRN_HEADS)
    forget = lb + (1.0 - lb) * jax.nn.sigmoid(f_logit)
    key = (1.0 - lb) * jax.nn.sigmoid(-f_logit)
    o = hgrn2_chunked(jax.nn.silu(to_heads(q, HGRN_HEADS)), key, to_heads(i, HGRN_HEADS), jnp.log(forget))
    o = head_rms_norm(o, norm_w) * jax.nn.silu(to_heads(g, HGRN_HEADS))
    return from_heads(o)


def hybrid_token_mixer(n, w_in, conv_w, a_log, dt_bias, gdn_norm_w, lower_bound, hgrn_norm_w, w_out):
    proj = n @ w_in
    o1 = 3 * GDN_WIDTH
    o2 = o1 + GDN_WIDTH
    o3 = o2 + GDN_HEADS
    o4 = o3 + GDN_HEADS
    o5 = o4 + HGRN_WIDTH
    o6 = o5 + HGRN_WIDTH
    o7 = o6 + HGRN_V_WIDTH
    qkv_a, z_a, a_a, b_a, q_b, f_b, i_b, g_b = jnp.split(proj, [o1, o2, o3, o4, o5, o6, o7], axis=-1)
    y_a = gated_deltanet_group(qkv_a, z_a, a_a, b_a, conv_w, a_log, dt_bias, gdn_norm_w)
    y_b = hgrn2_group(q_b, f_b, i_b, g_b, lower_bound, hgrn_norm_w)
    y = jnp.concatenate([y_a, y_b], axis=-1).astype(n.dtype)
    return y @ w_out


def squared_relu_mlp(n, w1, w2):
    return jnp.square(jax.nn.relu(n @ w1)) @ w2


def setup_inputs(seed: int = 0) -> dict:
    key = jax.random.key(seed)
    ks = jax.random.split(key, 16)
    f32 = jnp.float32
    x = jax.random.normal(ks[0], (BATCH, SEQ, D_MODEL), f32)
    w_in = jax.random.normal(ks[1], (DEPTH, D_MODEL, IN_PROJ_WIDTH), f32) * D_MODEL ** -0.5
    conv_w = jax.random.normal(ks[2], (DEPTH, CONV_WIDTH, 3 * GDN_WIDTH), f32) * CONV_WIDTH ** -0.5
    gdn_a_log = jnp.log(jax.random.uniform(ks[3], (DEPTH, GDN_HEADS), f32, 1.0, 16.0))
    dt = jnp.exp(jax.random.uniform(ks[4], (DEPTH, GDN_HEADS), f32, np.log(1e-3), np.log(1e-1)))
    gdn_dt_bias = dt + jnp.log(-jnp.expm1(-dt))
    gdn_norm_w = 1.0 + 0.02 * jax.random.normal(ks[5], (DEPTH, GDN_HEAD_DIM), f32)
    hgrn_lb_logits = 0.1 * jax.random.normal(ks[6], (DEPTH + 1, HGRN_WIDTH), f32)
    hgrn_norm_w = 1.0 + 0.02 * jax.random.normal(ks[7], (DEPTH, HGRN_VALUE_DIM), f32)
    w_out = jax.random.normal(ks[8], (DEPTH, D_MIX, D_MODEL), f32) * D_MIX ** -0.5
    norm_mix_w = 1.0 + 0.02 * jax.random.normal(ks[9], (DEPTH, D_MODEL), f32)
    norm_ffn_w = 1.0 + 0.02 * jax.random.normal(ks[10], (DEPTH, D_MODEL), f32)
    w_ff1 = jax.random.normal(ks[11], (DEPTH, D_MODEL, D_FF), f32) * D_MODEL ** -0.5
    w_ff2 = jax.random.normal(ks[12], (DEPTH, D_FF, D_MODEL), f32) * D_FF ** -0.5
    norm_final_w = 1.0 + 0.02 * jax.random.normal(ks[13], (D_MODEL,), f32)
    return {'x': x, 'w_in': w_in, 'conv_w': conv_w, 'gdn_a_log': gdn_a_log,
            'gdn_dt_bias': gdn_dt_bias, 'gdn_norm_w': gdn_norm_w, 'hgrn_lb_logits': hgrn_lb_logits,
            'hgrn_norm_w': hgrn_norm_w, 'w_out': w_out, 'norm_mix_w': norm_mix_w,
            'norm_ffn_w': norm_ffn_w, 'w_ff1': w_ff1, 'w_ff2': w_ff2, 'norm_final_w': norm_final_w}


def reference(x, w_in, conv_w, gdn_a_log, gdn_dt_bias, gdn_norm_w, hgrn_lb_logits, hgrn_norm_w,
              w_out, norm_mix_w, norm_ffn_w, w_ff1, w_ff2, norm_final_w):
    lower_bounds = jnp.cumsum(jax.nn.softmax(hgrn_lb_logits.astype(jnp.float32), axis=0), axis=0)
    h = x
    for layer in range(DEPTH):
        n = rms_norm(h, norm_mix_w[layer])
        h = h + hybrid_token_mixer(n, w_in[layer], conv_w[layer], gdn_a_log[layer], gdn_dt_bias[layer],
                                   gdn_norm_w[layer], lower_bounds[layer], hgrn_norm_w[layer], w_out[layer])
        n = rms_norm(h, norm_ffn_w[layer])
        h = h + squared_relu_mlp(n, w_ff1[layer], w_ff2[layer])
    return rms_norm(h, norm_final_w)
```

```python
import functools

import jax
import jax.numpy as jnp
from jax import lax
from jax.experimental import pallas as pl
from jax.experimental.pallas import tpu as pltpu

HEAD_DIM = 128
CHUNK = 64
HGRN_DIRECT = 8
CONV_WIDTH = 4
NORM_EPS = 1e-6
L2_EPS = 1e-6
VMEM_LIMIT_BYTES = 56 * 1024 * 1024

F32 = jnp.float32
BF16 = jnp.bfloat16


def _dot(a, b):
    return jnp.dot(a.astype(BF16), b.astype(BF16), preferred_element_type=F32)


def _dot_nt(a, b):
    return lax.dot_general(a.astype(BF16), b.astype(BF16), (((1,), (1,)), ((), ())),
                           preferred_element_type=F32)


def _dot_tn(a, b):
    return lax.dot_general(a.astype(BF16), b.astype(BF16), (((0,), (0,)), ((), ())),
                           preferred_element_type=F32)


def _dot_exact_lhs(a_bf16, x):
    x1 = x.astype(BF16)
    r1 = x - x1.astype(F32)
    x2 = r1.astype(BF16)
    x3 = (r1 - x2.astype(F32)).astype(BF16)
    d = functools.partial(jnp.dot, preferred_element_type=F32)
    return d(a_bf16, x1) + d(a_bf16, x2) + d(a_bf16, x3)


def _sigmoid(x):
    return 1.0 / (1.0 + jnp.exp(-x))


def _silu(x):
    return x * _sigmoid(x)


def _pad_rows(x):
    return jnp.concatenate([x, jnp.zeros_like(x)], axis=0)


def _head_norm_gate(o, nw, z):
    o = o * lax.rsqrt(jnp.mean(o * o, axis=-1, keepdims=True) + NORM_EPS) * nw
    return o * _silu(z)


def _inproj_kernel(x_ref, nw_ref, w_ref, wab_ref, o_ref, oab_ref, n_scr):
    @pl.when(pl.program_id(1) == 0)
    def _():
        x = x_ref[...]
        n = x * lax.rsqrt(jnp.mean(x * x, axis=-1, keepdims=True) + NORM_EPS) * nw_ref[...]
        nb = n.astype(BF16)
        n_scr[...] = nb
        oab_ref[...] = jnp.dot(nb, wab_ref[...], preferred_element_type=F32)

    o_ref[...] = jnp.dot(n_scr[...], w_ref[...], preferred_element_type=F32)


def _inproj(x2, norm_w, w_main, w_ab, tm, tn):
    m, d = x2.shape
    n = w_main.shape[1]
    return pl.pallas_call(
        _inproj_kernel,
        grid=(m // tm, n // tn),
        in_specs=[
            pl.BlockSpec((tm, d), lambda i, j: (i, 0)),
            pl.BlockSpec((1, d), lambda i, j: (0, 0)),
            pl.BlockSpec((d, tn), lambda i, j: (0, j)),
            pl.BlockSpec((d, HEAD_DIM), lambda i, j: (0, 0)),
        ],
        out_specs=[
            pl.BlockSpec((tm, tn), lambda i, j: (i, j)),
            pl.BlockSpec((tm, HEAD_DIM), lambda i, j: (i, 0)),
        ],
        out_shape=[jax.ShapeDtypeStruct((m, n), F32), jax.ShapeDtypeStruct((m, HEAD_DIM), F32)],
        scratch_shapes=[pltpu.VMEM((tm, d), BF16)],
        compiler_params=pltpu.CompilerParams(
            dimension_semantics=("arbitrary", "arbitrary"), vmem_limit_bytes=VMEM_LIMIT_BYTES),
    )(x2, norm_w.reshape(1, d), w_main, w_ab)


def _gdn_kernel(q_ref, k_ref, v_ref, z_ref, ab_ref, cwq_ref, cwk_ref, cwv_ref, gp_ref, nw_ref, y_ref,
                xq, xk, xv, s_scr, *, n_heads, tb):
    h = pl.program_id(1)

    @pl.when(pl.program_id(2) == 0)
    def _():
        zero8 = jnp.zeros((8, HEAD_DIM), F32)
        xq[0:8, :] = zero8
        xk[0:8, :] = zero8
        xv[0:8, :] = zero8
        s_scr[...] = jnp.zeros_like(s_scr)

    def conv_silu(x_ref, xs, cw_ref):
        xs[8:8 + tb, :] = x_ref[...]
        w = cw_ref[...]
        acc = xs[8:8 + tb, :] * w[CONV_WIDTH - 1:CONV_WIDTH, :]
        for j in range(CONV_WIDTH - 1):
            off = 8 - (CONV_WIDTH - 1) + j
            acc = acc + xs[off:off + tb, :] * w[j:j + 1, :]
        xs[0:8, :] = xs[tb:tb + 8, :]
        return _silu(acc)

    q_all = conv_silu(q_ref, xq, cwq_ref)
    k_all = conv_silu(k_ref, xk, cwk_ref)
    v_all = conv_silu(v_ref, xv, cwv_ref)

    ab = ab_ref[...]
    lane_t = lax.broadcasted_iota(jnp.int32, ab.shape, 1)
    a_b = jnp.broadcast_to(jnp.sum(jnp.where(lane_t == h, ab, 0.0), axis=-1, keepdims=True), ab.shape)
    b_b = jnp.broadcast_to(jnp.sum(jnp.where(lane_t == h + n_heads, ab, 0.0), axis=-1, keepdims=True),
                           ab.shape)
    gp = gp_ref[0]
    sp_in = a_b + gp[1:2, :]
    softplus = jnp.maximum(sp_in, 0.0) + jnp.log1p(jnp.exp(-jnp.abs(sp_in)))
    g_all = -jnp.exp(gp[0:1, :]) * softplus
    beta_all = _sigmoid(b_b)

    row = lax.broadcasted_iota(jnp.int32, (CHUNK, HEAD_DIM), 0)
    lane = lax.broadcasted_iota(jnp.int32, (CHUNK, HEAD_DIM), 1)
    col = lane & (CHUNK - 1)
    left = lane < CHUNK
    incl = row >= col
    strict = row > col
    blk16 = (row >> 4) == (col >> 4)
    blk32 = (row >> 5) == (col >> 5)
    r64 = lax.broadcasted_iota(jnp.int32, (CHUNK, CHUNK), 0)
    c64 = lax.broadcasted_iota(jnp.int32, (CHUNK, CHUNK), 1)
    tri_incl = (r64 >= c64).astype(BF16)
    eye_left = jnp.where(lane == row, 1.0, 0.0)
    zeros = jnp.zeros((CHUNK, HEAD_DIM), F32)
    nw = nw_ref[...]
    scale = HEAD_DIM ** -0.5

    s = s_scr[...]
    for c in range(tb // CHUNK):
        sl = slice(c * CHUNK, (c + 1) * CHUNK)
        g_c = g_all[sl]
        beta = beta_all[sl]
        dg = _dot_exact_lhs(tri_incl, jnp.concatenate([jnp.where(strict, g_c, 0.0), g_c], axis=1))
        gcum = dg[:, HEAD_DIM:]
        decay = jnp.where(incl, jnp.exp(dg[:, :HEAD_DIM]), 0.0)
        g_last = gcum[CHUNK - 1:CHUNK, :]
        e_g = jnp.exp(gcum)
        e_gl = jnp.exp(g_last - gcum)
        sd = jnp.exp(g_last)

        q = q_all[sl]
        k = k_all[sl]
        qn = q * (lax.rsqrt(jnp.sum(q * q, axis=-1, keepdims=True) + L2_EPS) * scale)
        kn = k * lax.rsqrt(jnp.sum(k * k, axis=-1, keepdims=True) + L2_EPS)
        qk = _dot_nt(jnp.concatenate([qn, kn], axis=0), jnp.concatenate([kn, kn], axis=0))
        attn = jnp.where(left, qk[:CHUNK] * decay, 0.0)
        lm = jnp.where(strict & (~left), beta * qk[CHUNK:] * decay, 0.0)

        t = eye_left - jnp.where(blk16, lm, 0.0)
        for _ in range(4):
            t = _dot(t, jnp.concatenate([zeros, t], axis=0)) + jnp.where(left, t, 0.0)
        inv = jnp.where(left, t, 0.0)
        l32 = jnp.where(blk32 & (~blk16), lm, 0.0)
        inv = inv - _dot(inv, _pad_rows(_dot(l32, jnp.concatenate([zeros, inv], axis=0))))
        l64 = jnp.where(blk32, 0.0, lm)
        inv = inv - _dot(inv, _pad_rows(_dot(l64, jnp.concatenate([zeros, inv], axis=0))))

        rhs = jnp.concatenate([beta * v_all[sl], beta * e_g * kn], axis=1)
        sol = _dot(inv, _pad_rows(rhs))
        u_v = sol[:, :HEAD_DIM]
        w = sol[:, HEAD_DIM:]

        ws = _dot(jnp.concatenate([w, qn * e_g], axis=0), s)
        u = u_v - ws[:CHUNK]
        o = ws[CHUNK:] + _dot(attn, _pad_rows(u))
        s = s * sd + _dot_tn(kn * e_gl, u)
        y_ref[sl, :] = _head_norm_gate(o, nw, z_ref[sl, :]).astype(y_ref.dtype)
    s_scr[...] = s


def _gdn(proj, ab, conv_w, gate_par, norm_w, batch, seq, n_heads, tb):
    m = proj.shape[0]
    nt = seq // tb
    tok = lambda off: pl.BlockSpec((tb, HEAD_DIM), lambda b, h, t: (b * nt + t, off + h))
    cw = lambda off: pl.BlockSpec((CONV_WIDTH, HEAD_DIM), lambda b, h, t: (0, off + h))
    return pl.pallas_call(
        functools.partial(_gdn_kernel, n_heads=n_heads, tb=tb),
        grid=(batch, n_heads, nt),
        in_specs=[
            tok(0), tok(n_heads), tok(2 * n_heads), tok(3 * n_heads),
            pl.BlockSpec((tb, HEAD_DIM), lambda b, h, t: (b * nt + t, 0)),
            cw(0), cw(n_heads), cw(2 * n_heads),
            pl.BlockSpec((1, 8, HEAD_DIM), lambda b, h, t: (h, 0, 0)),
            pl.BlockSpec((1, HEAD_DIM), lambda b, h, t: (0, 0)),
        ],
        out_specs=pl.BlockSpec((tb, HEAD_DIM), lambda b, h, t: (b * nt + t, h)),
        out_shape=jax.ShapeDtypeStruct((m, n_heads * HEAD_DIM), BF16),
        scratch_shapes=[pltpu.VMEM((tb + 8, HEAD_DIM), F32)] * 3 + [pltpu.VMEM((HEAD_DIM, HEAD_DIM), F32)],
        compiler_params=pltpu.CompilerParams(
            dimension_semantics=("arbitrary", "arbitrary", "arbitrary"), vmem_limit_bytes=VMEM_LIMIT_BYTES),
    )(proj, proj, proj, proj, ab, conv_w, conv_w, conv_w, gate_par, norm_w.reshape(1, HEAD_DIM))


def _hgrn_kernel(q_ref, f_ref, i_ref, g_ref, lb_ref, nw_ref, y_ref, st_scr, *, layer, tb):
    @pl.when(pl.program_id(2) == 0)
    def _():
        st_scr[...] = jnp.zeros_like(st_scr)

    logits = lb_ref[...]
    e = jnp.exp(logits - jnp.max(logits, axis=0, keepdims=True))
    lb = jnp.sum(e[:layer + 1], axis=0, keepdims=True) / jnp.sum(e, axis=0, keepdims=True)
    oml = 1.0 - lb

    row = lax.broadcasted_iota(jnp.int32, (CHUNK, HEAD_DIM), 0)
    lane = lax.broadcasted_iota(jnp.int32, (CHUNK, HEAD_DIM), 1)
    r64 = lax.broadcasted_iota(jnp.int32, (CHUNK, CHUNK), 0)
    c64 = lax.broadcasted_iota(jnp.int32, (CHUNK, CHUNK), 1)
    tri_incl = (r64 >= c64).astype(BF16)
    nw = nw_ref[...]

    st = st_scr[...]
    for c in range(tb // CHUNK):
        sl = slice(c * CHUNK, (c + 1) * CHUNK)
        fl = f_ref[sl, :]
        key = oml * _sigmoid(-fl)
        logf = jnp.log(lb + oml * _sigmoid(fl))
        qs = _silu(q_ref[sl, :])
        v = i_ref[sl, :]
        b = _dot_exact_lhs(tri_incl, logf)
        b_last = b[CHUNK - 1:CHUNK, :]

        o = _dot_nt(qs * jnp.exp(b), st)

        a = jnp.zeros((CHUNK, HEAD_DIM), F32)
        n = CHUNK // 2
        while n >= HGRN_DIRECT:
            upper = (row & n) != 0
            b_ref = jnp.concatenate(
                [jnp.broadcast_to(b[r0 + n - 1:r0 + n, :], (2 * n, HEAD_DIM)) for r0 in range(0, CHUNK, 2 * n)],
                axis=0)
            ex = jnp.exp(jnp.where(upper, b - b_ref, b_ref - b))
            q_rel = jnp.where(upper, qs * ex, 0.0)
            k_rel = jnp.where(upper, 0.0, key * ex)
            a_n = _dot_nt(q_rel, _pad_rows(k_rel))
            a = a + jnp.where((row & -(2 * n)) == (lane & -(2 * n)), a_n, 0.0)
            n //= 2

        od = jnp.sum(qs * key, axis=-1, keepdims=True) * v
        for delta in range(1, HGRN_DIRECT):
            valid = (row & (HGRN_DIRECT - 1)) >= delta
            k_sh = pltpu.roll(key, delta, 0)
            b_sh = pltpu.roll(b, delta, 0)
            v_sh = pltpu.roll(v, delta, 0)
            p = jnp.where(valid, qs * k_sh * jnp.exp(jnp.where(valid, b - b_sh, 0.0)), 0.0)
            od = od + jnp.sum(p, axis=-1, keepdims=True) * v_sh

        o = o + _dot(a, _pad_rows(v)) + od
        st = st * jnp.exp(b_last) + _dot_tn(v, key * jnp.exp(b_last - b))
        y_ref[sl, :] = _head_norm_gate(o, nw, g_ref[sl, :]).astype(y_ref.dtype)
    st_scr[...] = st


def _hgrn(proj, lb_logits, norm_w, batch, seq, n_heads, col0, layer, tb):
    m = proj.shape[0]
    nt = seq // tb
    tok = lambda off: pl.BlockSpec((tb, HEAD_DIM), lambda b, h, t: (b * nt + t, col0 + off + h))
    return pl.pallas_call(
        functools.partial(_hgrn_kernel, layer=layer, tb=tb),
        grid=(batch, n_heads, nt),
        in_specs=[
            tok(0), tok(n_heads), tok(2 * n_heads), tok(3 * n_heads),
            pl.BlockSpec((lb_logits.shape[0], HEAD_DIM), lambda b, h, t: (0, h)),
            pl.BlockSpec((1, HEAD_DIM), lambda b, h, t: (0, 0)),
        ],
        out_specs=pl.BlockSpec((tb, HEAD_DIM), lambda b, h, t: (b * nt + t, h)),
        out_shape=jax.ShapeDtypeStruct((m, n_heads * HEAD_DIM), BF16),
        scratch_shapes=[pltpu.VMEM((HEAD_DIM, HEAD_DIM), F32)],
        compiler_params=pltpu.CompilerParams(
            dimension_semantics=("arbitrary", "arbitrary", "arbitrary"), vmem_limit_bytes=VMEM_LIMIT_BYTES),
    )(proj, proj, proj, proj, lb_logits, norm_w.reshape(1, HEAD_DIM))


def _outproj_kernel(ya_ref, yb_ref, wa_ref, wb_ref, x_ref, o_ref):
    acc = jnp.dot(ya_ref[...], wa_ref[...], preferred_element_type=F32)
    acc = acc + jnp.dot(yb_ref[...], wb_ref[...], preferred_element_type=F32)
    o_ref[...] = x_ref[...] + acc


def _outproj(ya, yb, w_a, w_b, x2, tm):
    m, d = x2.shape
    return pl.pallas_call(
        _outproj_kernel,
        grid=(m // tm,),
        in_specs=[
            pl.BlockSpec((tm, ya.shape[1]), lambda i: (i, 0)),
            pl.BlockSpec((tm, yb.shape[1]), lambda i: (i, 0)),
            pl.BlockSpec(w_a.shape, lambda i: (0, 0)),
            pl.BlockSpec(w_b.shape, lambda i: (0, 0)),
            pl.BlockSpec((tm, d), lambda i: (i, 0)),
        ],
        out_specs=pl.BlockSpec((tm, d), lambda i: (i, 0)),
        out_shape=jax.ShapeDtypeStruct((m, d), F32),
        compiler_params=pltpu.CompilerParams(
            dimension_semantics=("arbitrary",), vmem_limit_bytes=VMEM_LIMIT_BYTES),
    )(ya, yb, w_a, w_b, x2)


def _mlp_kernel(h_ref, nw_ref, w1_ref, w2_ref, fw_ref, o_ref, n_scr, *, final_norm):
    f = pl.program_id(1)

    @pl.when(f == 0)
    def _():
        x = h_ref[...]
        n = x * lax.rsqrt(jnp.mean(x * x, axis=-1, keepdims=True) + NORM_EPS) * nw_ref[...]
        n_scr[...] = n.astype(BF16)
        o_ref[...] = x

    hid = jnp.dot(n_scr[...], w1_ref[...], preferred_element_type=F32)
    hid = jnp.square(jnp.maximum(hid, 0.0)).astype(BF16)
    o_ref[...] += jnp.dot(hid, w2_ref[...], preferred_element_type=F32)

    if final_norm:
        @pl.when(f == pl.num_programs(1) - 1)
        def _():
            y = o_ref[...]
            o_ref[...] = y * lax.rsqrt(jnp.mean(y * y, axis=-1, keepdims=True) + NORM_EPS) * fw_ref[...]


def _mlp(h, norm_w, w1, w2, final_w, final_norm, tm, tf):
    m, d = h.shape
    ff = w1.shape[1]
    return pl.pallas_call(
        functools.partial(_mlp_kernel, final_norm=final_norm),
        grid=(m // tm, ff // tf),
        in_specs=[
            pl.BlockSpec((tm, d), lambda i, f: (i, 0)),
            pl.BlockSpec((1, d), lambda i, f: (0, 0)),
            pl.BlockSpec((d, tf), lambda i, f: (0, f)),
            pl.BlockSpec((tf, d), lambda i, f: (f, 0)),
            pl.BlockSpec((1, d), lambda i, f: (0, 0)),
        ],
        out_specs=pl.BlockSpec((tm, d), lambda i, f: (i, 0)),
        out_shape=jax.ShapeDtypeStruct((m, d), F32),
        scratch_shapes=[pltpu.VMEM((tm, d), BF16)],
        compiler_params=pltpu.CompilerParams(
            dimension_semantics=("arbitrary", "arbitrary"), vmem_limit_bytes=VMEM_LIMIT_BYTES),
    )(h, norm_w.reshape(1, d), w1, w2, final_w.reshape(1, d))


def _tile(n, target):
    t = min(n, target)
    while n % t:
        t //= 2
    return t


def kernel(x, w_in, conv_w, gdn_a_log, gdn_dt_bias, gdn_norm_w, hgrn_lb_logits, hgrn_norm_w, w_out,
           norm_mix_w, norm_ffn_w, w_ff1, w_ff2, norm_final_w):
    batch, seq, d_model = x.shape
    depth = w_in.shape[0]
    gh = gdn_a_log.shape[1]
    hh = hgrn_lb_logits.shape[1] // HEAD_DIM
    gw, hw = gh * HEAD_DIM, hh * HEAD_DIM
    assert 2 * gh <= HEAD_DIM and seq % CHUNK == 0
    assert w_in.shape[2] == 4 * gw + 2 * gh + 4 * hw

    m = batch * seq
    tm = _tile(m, 512)
    tb = _tile(seq, 256)
    h = x.reshape(m, d_model)
    for layer in range(depth):
        wl = w_in[layer]
        o1 = 4 * gw
        o2 = o1 + 2 * gh
        w_main = jnp.concatenate([wl[:, :o1], wl[:, o2:]], axis=1).astype(BF16)
        w_ab = jnp.pad(wl[:, o1:o2], ((0, 0), (0, HEAD_DIM - 2 * gh))).astype(BF16)
        gate_par = jnp.zeros((gh, 8, HEAD_DIM), F32)
        gate_par = gate_par.at[:, 0, :].set(gdn_a_log[layer][:, None])
        gate_par = gate_par.at[:, 1, :].set(gdn_dt_bias[layer][:, None])
        w_o = w_out[layer].astype(BF16)

        proj, ab = _inproj(h, norm_mix_w[layer], w_main, w_ab, tm, _tile(w_main.shape[1], 1024))
        y_a = _gdn(proj, ab, conv_w[layer], gate_par, gdn_norm_w[layer], batch, seq, gh, tb)
        y_b = _hgrn(proj, hgrn_lb_logits, hgrn_norm_w[layer], batch, seq, hh, 4 * gh, layer, tb)
        h = _outproj(y_a, y_b, w_o[:gw], w_o[gw:], h, tm)
        h = _mlp(h, norm_ffn_w[layer], w_ff1[layer].astype(BF16), w_ff2[layer].astype(BF16), norm_final_w,
                 layer == depth - 1, tm, _tile(w_ff1.shape[2], 1024))
    return h.reshape(batch, seq, d_model)
```

```python
import functools

import jax
import jax.numpy as jnp
from jax import lax
from jax.experimental import pallas as pl
from jax.experimental.pallas import tpu as pltpu

HEAD_DIM = 128
CHUNK = 64
HGRN_DIRECT = 8
CONV_WIDTH = 4
NORM_EPS = 1e-6
L2_EPS = 1e-6
VMEM_LIMIT_BYTES = 56 * 1024 * 1024

GDN_BLOCK = (8, 128)
HGRN_BLOCK = (2, 256)
ROW_TILE = 512
COL_TILE = 1024

F32 = jnp.float32
BF16 = jnp.bfloat16


def _dot(a, b):
    return jnp.dot(a.astype(BF16), b.astype(BF16), preferred_element_type=F32)


def _dot_nt(a, b):
    return lax.dot_general(a.astype(BF16), b.astype(BF16), (((1,), (1,)), ((), ())),
                           preferred_element_type=F32)


def _dot_tn(a, b):
    return lax.dot_general(a.astype(BF16), b.astype(BF16), (((0,), (0,)), ((), ())),
                           preferred_element_type=F32)


def _dot_exact_lhs(a_bf16, x):
    x1 = x.astype(BF16)
    r1 = x - x1.astype(F32)
    x2 = r1.astype(BF16)
    x3 = (r1 - x2.astype(F32)).astype(BF16)
    d = functools.partial(jnp.dot, preferred_element_type=F32)
    return d(a_bf16, x1) + d(a_bf16, x2) + d(a_bf16, x3)


def _sigmoid(x):
    return 1.0 / (1.0 + jnp.exp(-x))


def _silu(x):
    return x * _sigmoid(x)


def _pad_rows(x):
    return jnp.concatenate([x, jnp.zeros_like(x)], axis=0)


def _head_norm_gate(o, nw, z):
    o = o * lax.rsqrt(jnp.mean(o * o, axis=-1, keepdims=True) + NORM_EPS) * nw
    return o * _silu(z)


def _inproj_kernel(x_ref, nw_ref, w_ref, wab_ref, o_ref, oab_ref, n_scr):
    @pl.when(pl.program_id(1) == 0)
    def _():
        x = x_ref[...]
        n = x * lax.rsqrt(jnp.mean(x * x, axis=-1, keepdims=True) + NORM_EPS) * nw_ref[...]
        nb = n.astype(BF16)
        n_scr[...] = nb
        oab_ref[...] = jnp.dot(nb, wab_ref[...], preferred_element_type=F32)

    o_ref[...] = jnp.dot(n_scr[...], w_ref[...], preferred_element_type=F32)


def _inproj(x2, norm_w, w_main, w_ab, tm, tn):
    m, d = x2.shape
    n = w_main.shape[1]
    return pl.pallas_call(
        _inproj_kernel,
        grid=(m // tm, n // tn),
        in_specs=[
            pl.BlockSpec((tm, d), lambda i, j: (i, 0)),
            pl.BlockSpec((1, d), lambda i, j: (0, 0)),
            pl.BlockSpec((d, tn), lambda i, j: (0, j)),
            pl.BlockSpec((d, HEAD_DIM), lambda i, j: (0, 0)),
        ],
        out_specs=[
            pl.BlockSpec((tm, tn), lambda i, j: (i, j)),
            pl.BlockSpec((tm, HEAD_DIM), lambda i, j: (i, 0)),
        ],
        out_shape=[jax.ShapeDtypeStruct((m, n), F32), jax.ShapeDtypeStruct((m, HEAD_DIM), F32)],
        scratch_shapes=[pltpu.VMEM((tm, d), BF16)],
        compiler_params=pltpu.CompilerParams(
            dimension_semantics=("arbitrary", "arbitrary"), vmem_limit_bytes=VMEM_LIMIT_BYTES),
    )(x2, norm_w.reshape(1, d), w_main, w_ab)


def _gdn_kernel(q_ref, k_ref, v_ref, z_ref, ab_ref, cwq_ref, cwk_ref, cwv_ref, gp_ref, nw_ref, y_ref,
                xq, xk, xv, s_scr, *, n_heads, hb, tb):
    h0 = pl.program_id(1) * hb

    @pl.when(pl.program_id(2) == 0)
    def _():
        zero8 = jnp.zeros((8, hb * HEAD_DIM), F32)
        xq[0:8, :] = zero8
        xk[0:8, :] = zero8
        xv[0:8, :] = zero8
        s_scr[...] = jnp.zeros_like(s_scr)

    def conv_silu(x_ref, xs, cw_ref):
        xs[8:8 + tb, :] = x_ref[...]
        w = cw_ref[...]
        acc = xs[8:8 + tb, :] * w[CONV_WIDTH - 1:CONV_WIDTH, :]
        for j in range(CONV_WIDTH - 1):
            off = 8 - (CONV_WIDTH - 1) + j
            acc = acc + xs[off:off + tb, :] * w[j:j + 1, :]
        xs[0:8, :] = xs[tb:tb + 8, :]
        return _silu(acc)

    q_all = conv_silu(q_ref, xq, cwq_ref)
    k_all = conv_silu(k_ref, xk, cwk_ref)
    v_all = conv_silu(v_ref, xv, cwv_ref)

    ab = ab_ref[...]
    lane_t = lax.broadcasted_iota(jnp.int32, ab.shape, 1)

    def gate_column(idx):
        return jnp.broadcast_to(jnp.sum(jnp.where(lane_t == idx, ab, 0.0), axis=-1, keepdims=True), ab.shape)

    g_heads, beta_heads = [], []
    for j in range(hb):
        gp = gp_ref[j]
        sp_in = gate_column(h0 + j) + gp[1:2, :]
        softplus = jnp.maximum(sp_in, 0.0) + jnp.log1p(jnp.exp(-jnp.abs(sp_in)))
        g_heads.append(-jnp.exp(gp[0:1, :]) * softplus)
        beta_heads.append(_sigmoid(gate_column(h0 + j + n_heads)))

    row = lax.broadcasted_iota(jnp.int32, (CHUNK, HEAD_DIM), 0)
    lane = lax.broadcasted_iota(jnp.int32, (CHUNK, HEAD_DIM), 1)
    col = lane & (CHUNK - 1)
    left = lane < CHUNK
    incl = row >= col
    strict = row > col
    blk16 = (row >> 4) == (col >> 4)
    blk32 = (row >> 5) == (col >> 5)
    r64 = lax.broadcasted_iota(jnp.int32, (CHUNK, CHUNK), 0)
    c64 = lax.broadcasted_iota(jnp.int32, (CHUNK, CHUNK), 1)
    tri_incl = (r64 >= c64).astype(BF16)
    eye_left = jnp.where(lane == row, 1.0, 0.0)
    zeros = jnp.zeros((CHUNK, HEAD_DIM), F32)
    nw = nw_ref[...]
    scale = HEAD_DIM ** -0.5

    n_chunks = tb // CHUNK
    items = [(j, c) for c in range(n_chunks) for j in range(hb)]
    rs = [slice(c * CHUNK, (c + 1) * CHUNK) for _, c in items]
    hs = [slice(j * HEAD_DIM, (j + 1) * HEAD_DIM) for j, _ in items]
    betas = [beta_heads[j][r] for (j, _), r in zip(items, rs)]
    gs = [g_heads[j][r] for (j, _), r in zip(items, rs)]
    dgs = [_dot_exact_lhs(tri_incl, jnp.concatenate([jnp.where(strict, g, 0.0), g], axis=1)) for g in gs]
    gcums = [dg[:, HEAD_DIM:] for dg in dgs]
    decays = [jnp.where(incl, jnp.exp(dg[:, :HEAD_DIM]), 0.0) for dg in dgs]
    g_lasts = [g[CHUNK - 1:CHUNK, :] for g in gcums]
    e_gs = [jnp.exp(g) for g in gcums]
    sds = [jnp.exp(gl) for gl in g_lasts]

    def l2n(x):
        return x * lax.rsqrt(jnp.sum(x * x, axis=-1, keepdims=True) + L2_EPS)

    qns = [l2n(q_all[r, h]) * scale for r, h in zip(rs, hs)]
    kns = [l2n(k_all[r, h]) for r, h in zip(rs, hs)]
    qks = [_dot_nt(jnp.concatenate([qn, kn], axis=0), jnp.concatenate([kn, kn], axis=0)) for qn, kn in zip(qns, kns)]
    attns = [jnp.where(left, qk[:CHUNK] * d, 0.0) for qk, d in zip(qks, decays)]
    lms = [jnp.where(strict & (~left), b * qk[CHUNK:] * d, 0.0) for b, qk, d in zip(betas, qks, decays)]

    ts = [eye_left - jnp.where(blk16, lm, 0.0) for lm in lms]
    for _ in range(4):
        ts = [_dot(t, jnp.concatenate([zeros, t], axis=0)) + jnp.where(left, t, 0.0) for t in ts]
    invs = [jnp.where(left, t, 0.0) for t in ts]
    for sel in (blk32 & (~blk16), ~blk32):
        w1s = [_dot(jnp.where(sel, lm, 0.0), jnp.concatenate([zeros, inv], axis=0)) for lm, inv in zip(lms, invs)]
        invs = [inv - _dot(inv, _pad_rows(w1)) for inv, w1 in zip(invs, w1s)]

    sols = [_dot(inv, _pad_rows(jnp.concatenate([b * e_g * kn, b * v_all[r, h]], axis=1)))
            for inv, b, e_g, kn, r, h in zip(invs, betas, e_gs, kns, rs, hs)]
    k_ends = [kn * jnp.exp(gl - g) for kn, gl, g in zip(kns, g_lasts, gcums)]
    kws = [_dot_tn(k_end, sol) for k_end, sol in zip(k_ends, sols)]
    aws = [_dot(attn, _pad_rows(sol)) for attn, sol in zip(attns, sols)]
    lhss = [jnp.concatenate([kw[:, :HEAD_DIM], qn * e_g - aw[:, :HEAD_DIM]], axis=0)
            for kw, qn, e_g, aw in zip(kws, qns, e_gs, aws)]

    states = [s_scr[j] for j in range(hb)]
    for i, (j, _) in enumerate(items):
        r = _dot(lhss[i], states[j])
        o = r[HEAD_DIM:] + aws[i][:, HEAD_DIM:]
        states[j] = states[j] * sds[i] - r[:HEAD_DIM] + kws[i][:, HEAD_DIM:]
        y_ref[rs[i], hs[i]] = _head_norm_gate(o, nw, z_ref[rs[i], hs[i]]).astype(y_ref.dtype)
    for j in range(hb):
        s_scr[j] = states[j]


def _gdn(proj, ab, conv_w, gate_par, norm_w, batch, seq, n_heads, hb, tb):
    m = proj.shape[0]
    nt = seq // tb
    ng = n_heads // hb
    width = hb * HEAD_DIM
    tok = lambda part: pl.BlockSpec((tb, width), lambda b, g, t: (b * nt + t, part * ng + g))
    cw = lambda part: pl.BlockSpec((CONV_WIDTH, width), lambda b, g, t: (0, part * ng + g))
    return pl.pallas_call(
        functools.partial(_gdn_kernel, n_heads=n_heads, hb=hb, tb=tb),
        grid=(batch, ng, nt),
        in_specs=[
            tok(0), tok(1), tok(2), tok(3),
            pl.BlockSpec((tb, HEAD_DIM), lambda b, g, t: (b * nt + t, 0)),
            cw(0), cw(1), cw(2),
            pl.BlockSpec((hb, 8, HEAD_DIM), lambda b, g, t: (g, 0, 0)),
            pl.BlockSpec((1, HEAD_DIM), lambda b, g, t: (0, 0)),
        ],
        out_specs=pl.BlockSpec((tb, width), lambda b, g, t: (b * nt + t, g)),
        out_shape=jax.ShapeDtypeStruct((m, n_heads * HEAD_DIM), BF16),
        scratch_shapes=[pltpu.VMEM((tb + 8, width), F32)] * 3 + [pltpu.VMEM((hb, HEAD_DIM, HEAD_DIM), F32)],
        compiler_params=pltpu.CompilerParams(
            dimension_semantics=("arbitrary", "arbitrary", "arbitrary"), vmem_limit_bytes=VMEM_LIMIT_BYTES),
    )(proj, proj, proj, proj, ab, conv_w, conv_w, conv_w, gate_par, norm_w.reshape(1, HEAD_DIM))


def _hgrn_kernel(q_ref, f_ref, i_ref, g_ref, lb_ref, nw_ref, y_ref, st_scr, *, layer, hb, tb):
    @pl.when(pl.program_id(2) == 0)
    def _():
        st_scr[...] = jnp.zeros_like(st_scr)

    logits = lb_ref[...]
    e = jnp.exp(logits - jnp.max(logits, axis=0, keepdims=True))
    lb_all = jnp.sum(e[:layer + 1], axis=0, keepdims=True) / jnp.sum(e, axis=0, keepdims=True)

    row = lax.broadcasted_iota(jnp.int32, (CHUNK, HEAD_DIM), 0)
    lane = lax.broadcasted_iota(jnp.int32, (CHUNK, HEAD_DIM), 1)
    r64 = lax.broadcasted_iota(jnp.int32, (CHUNK, CHUNK), 0)
    c64 = lax.broadcasted_iota(jnp.int32, (CHUNK, CHUNK), 1)
    tri_incl = (r64 >= c64).astype(BF16)
    nw = nw_ref[...]

    n_chunks = tb // CHUNK
    items = [(j, c) for c in range(n_chunks) for j in range(hb)]
    ids = range(len(items))
    rs = [slice(c * CHUNK, (c + 1) * CHUNK) for _, c in items]
    hs = [slice(j * HEAD_DIM, (j + 1) * HEAD_DIM) for j, _ in items]
    lbs = [lb_all[:, h] for h in hs]
    keys = [(1.0 - lb) * _sigmoid(-f_ref[r, h]) for lb, r, h in zip(lbs, rs, hs)]
    qss = [_silu(q_ref[r, h]) for r, h in zip(rs, hs)]
    vs = [i_ref[r, h] for r, h in zip(rs, hs)]
    bs = [_dot_exact_lhs(tri_incl, jnp.log(lb + (1.0 - lb) * _sigmoid(f_ref[r, h]))) for lb, r, h in zip(lbs, rs, hs)]
    b_lasts = [b[CHUNK - 1:CHUNK, :] for b in bs]

    a_s = [jnp.zeros((CHUNK, HEAD_DIM), F32) for _ in ids]
    n = CHUNK // 2
    while n >= HGRN_DIRECT:
        upper = (row & n) != 0
        same = (row & -(2 * n)) == (lane & -(2 * n))
        for i in ids:
            b = bs[i]
            b_ref = jnp.concatenate(
                [jnp.broadcast_to(b[r0 + n - 1:r0 + n, :], (2 * n, HEAD_DIM)) for r0 in range(0, CHUNK, 2 * n)],
                axis=0)
            ex = jnp.exp(jnp.where(upper, b - b_ref, b_ref - b))
            q_rel = jnp.where(upper, qss[i] * ex, 0.0)
            k_rel = jnp.where(upper, 0.0, keys[i] * ex)
            a_s[i] = a_s[i] + jnp.where(same, _dot_nt(q_rel, _pad_rows(k_rel)), 0.0)
        n //= 2

    ods = [jnp.sum(qs * key, axis=-1, keepdims=True) * v for qs, key, v in zip(qss, keys, vs)]
    for delta in range(1, HGRN_DIRECT):
        valid = (row & (HGRN_DIRECT - 1)) >= delta
        for i in ids:
            k_sh = pltpu.roll(keys[i], delta, 0)
            b_sh = pltpu.roll(bs[i], delta, 0)
            v_sh = pltpu.roll(vs[i], delta, 0)
            p = jnp.where(valid, qss[i] * k_sh * jnp.exp(jnp.where(valid, bs[i] - b_sh, 0.0)), 0.0)
            ods[i] = ods[i] + jnp.sum(p, axis=-1, keepdims=True) * v_sh

    intras = [_dot(a, _pad_rows(v)) + od for a, v, od in zip(a_s, vs, ods)]
    kvs = [_dot_tn(v, key * jnp.exp(bl - b)) for v, key, bl, b in zip(vs, keys, b_lasts, bs)]
    qis = [qs * jnp.exp(b) for qs, b in zip(qss, bs)]

    states = [st_scr[j] for j in range(hb)]
    for i, (j, _) in enumerate(items):
        o = _dot_nt(qis[i], states[j]) + intras[i]
        states[j] = states[j] * jnp.exp(b_lasts[i]) + kvs[i]
        y_ref[rs[i], hs[i]] = _head_norm_gate(o, nw, g_ref[rs[i], hs[i]]).astype(y_ref.dtype)
    for j in range(hb):
        st_scr[j] = states[j]


def _hgrn(proj, lb_logits, norm_w, batch, seq, n_heads, col0, layer, hb, tb):
    m = proj.shape[0]
    nt = seq // tb
    ng = n_heads // hb
    width = hb * HEAD_DIM
    tok = lambda part: pl.BlockSpec((tb, width), lambda b, g, t: (b * nt + t, col0 // hb + part * ng + g))
    return pl.pallas_call(
        functools.partial(_hgrn_kernel, layer=layer, hb=hb, tb=tb),
        grid=(batch, ng, nt),
        in_specs=[
            tok(0), tok(1), tok(2), tok(3),
            pl.BlockSpec((lb_logits.shape[0], width), lambda b, g, t: (0, g)),
            pl.BlockSpec((1, HEAD_DIM), lambda b, g, t: (0, 0)),
        ],
        out_specs=pl.BlockSpec((tb, width), lambda b, g, t: (b * nt + t, g)),
        out_shape=jax.ShapeDtypeStruct((m, n_heads * HEAD_DIM), BF16),
        scratch_shapes=[pltpu.VMEM((hb, HEAD_DIM, HEAD_DIM), F32)],
        compiler_params=pltpu.CompilerParams(
            dimension_semantics=("arbitrary", "arbitrary", "arbitrary"), vmem_limit_bytes=VMEM_LIMIT_BYTES),
    )(proj, proj, proj, proj, lb_logits, norm_w.reshape(1, HEAD_DIM))


def _outproj_kernel(ya_ref, yb_ref, wa_ref, wb_ref, x_ref, o_ref):
    acc = jnp.dot(ya_ref[...], wa_ref[...], preferred_element_type=F32)
    acc = acc + jnp.dot(yb_ref[...], wb_ref[...], preferred_element_type=F32)
    o_ref[...] = x_ref[...] + acc


def _outproj(ya, yb, w_a, w_b, x2, tm):
    m, d = x2.shape
    return pl.pallas_call(
        _outproj_kernel,
        grid=(m // tm,),
        in_specs=[
            pl.BlockSpec((tm, ya.shape[1]), lambda i: (i, 0)),
            pl.BlockSpec((tm, yb.shape[1]), lambda i: (i, 0)),
            pl.BlockSpec(w_a.shape, lambda i: (0, 0)),
            pl.BlockSpec(w_b.shape, lambda i: (0, 0)),
            pl.BlockSpec((tm, d), lambda i: (i, 0)),
        ],
        out_specs=pl.BlockSpec((tm, d), lambda i: (i, 0)),
        out_shape=jax.ShapeDtypeStruct((m, d), F32),
        compiler_params=pltpu.CompilerParams(
            dimension_semantics=("arbitrary",), vmem_limit_bytes=VMEM_LIMIT_BYTES),
    )(ya, yb, w_a, w_b, x2)


def _mlp_kernel(h_ref, nw_ref, w1_ref, w2_ref, fw_ref, o_ref, n_scr, *, final_norm):
    f = pl.program_id(1)

    @pl.when(f == 0)
    def _():
        x = h_ref[...]
        n = x * lax.rsqrt(jnp.mean(x * x, axis=-1, keepdims=True) + NORM_EPS) * nw_ref[...]
        n_scr[...] = n.astype(BF16)
        o_ref[...] = x

    hid = jnp.dot(n_scr[...], w1_ref[...], preferred_element_type=F32)
    hid = jnp.square(jnp.maximum(hid, 0.0)).astype(BF16)
    o_ref[...] += jnp.dot(hid, w2_ref[...], preferred_element_type=F32)

    if final_norm:
        @pl.when(f == pl.num_programs(1) - 1)
        def _():
            y = o_ref[...]
            o_ref[...] = y * lax.rsqrt(jnp.mean(y * y, axis=-1, keepdims=True) + NORM_EPS) * fw_ref[...]


def _mlp(h, norm_w, w1, w2, final_w, final_norm, tm, tf):
    m, d = h.shape
    ff = w1.shape[1]
    return pl.pallas_call(
        functools.partial(_mlp_kernel, final_norm=final_norm),
        grid=(m // tm, ff // tf),
        in_specs=[
            pl.BlockSpec((tm, d), lambda i, f: (i, 0)),
            pl.BlockSpec((1, d), lambda i, f: (0, 0)),
            pl.BlockSpec((d, tf), lambda i, f: (0, f)),
            pl.BlockSpec((tf, d), lambda i, f: (f, 0)),
            pl.BlockSpec((1, d), lambda i, f: (0, 0)),
        ],
        out_specs=pl.BlockSpec((tm, d), lambda i, f: (i, 0)),
        out_shape=jax.ShapeDtypeStruct((m, d), F32),
        scratch_shapes=[pltpu.VMEM((tm, d), BF16)],
        compiler_params=pltpu.CompilerParams(
            dimension_semantics=("arbitrary", "arbitrary"), vmem_limit_bytes=VMEM_LIMIT_BYTES),
    )(h, norm_w.reshape(1, d), w1, w2, final_w.reshape(1, d))


def _tile(n, target):
    t = min(n, target)
    while n % t:
        t //= 2
    return t


def kernel(x, w_in, conv_w, gdn_a_log, gdn_dt_bias, gdn_norm_w, hgrn_lb_logits, hgrn_norm_w, w_out,
           norm_mix_w, norm_ffn_w, w_ff1, w_ff2, norm_final_w):
    batch, seq, d_model = x.shape
    depth = w_in.shape[0]
    gh = gdn_a_log.shape[1]
    hh = hgrn_lb_logits.shape[1] // HEAD_DIM
    gw, hw = gh * HEAD_DIM, hh * HEAD_DIM
    assert 2 * gh <= HEAD_DIM and seq % CHUNK == 0
    assert w_in.shape[2] == 4 * gw + 2 * gh + 4 * hw

    m = batch * seq
    tm = _tile(m, ROW_TILE)
    gdn_hb, gdn_tb = _tile(gh, GDN_BLOCK[0]), _tile(seq, GDN_BLOCK[1])
    hgrn_hb, hgrn_tb = _tile(hh, HGRN_BLOCK[0]), _tile(seq, HGRN_BLOCK[1])
    assert (4 * gh) % hgrn_hb == 0
    h = x.reshape(m, d_model)
    for layer in range(depth):
        wl = w_in[layer]
        o1 = 4 * gw
        o2 = o1 + 2 * gh
        w_main = jnp.concatenate([wl[:, :o1], wl[:, o2:]], axis=1).astype(BF16)
        w_ab = jnp.pad(wl[:, o1:o2], ((0, 0), (0, HEAD_DIM - 2 * gh))).astype(BF16)
        gate_par = jnp.zeros((gh, 8, HEAD_DIM), F32)
        gate_par = gate_par.at[:, 0, :].set(gdn_a_log[layer][:, None])
        gate_par = gate_par.at[:, 1, :].set(gdn_dt_bias[layer][:, None])
        w_o = w_out[layer].astype(BF16)

        proj, ab = _inproj(h, norm_mix_w[layer], w_main, w_ab, tm, _tile(w_main.shape[1], COL_TILE))
        y_a = _gdn(proj, ab, conv_w[layer], gate_par, gdn_norm_w[layer], batch, seq, gh, gdn_hb, gdn_tb)
        y_b = _hgrn(proj, hgrn_lb_logits, hgrn_norm_w[layer], batch, seq, hh, 4 * gh, layer, hgrn_hb, hgrn_tb)
        h = _outproj(y_a, y_b, w_o[:gw], w_o[gw:], h, tm)
        h = _mlp(h, norm_ffn_w[layer], w_ff1[layer].astype(BF16), w_ff2[layer].astype(BF16), norm_final_w,
                 layer == depth - 1, tm, _tile(w_ff1.shape[2], COL_TILE))
    return h.reshape(batch, seq, d_model)
```

```python
import functools

import jax
import jax.numpy as jnp
from jax import lax
from jax.experimental import pallas as pl
from jax.experimental.pallas import tpu as pltpu

HEAD_DIM = 128
CHUNK = 64
CONV_WIDTH = 4
NORM_EPS = 1e-6
L2_EPS = 1e-6
VMEM_LIMIT_BYTES = 56 * 1024 * 1024

GDN_BLOCK = (8, 128)
HGRN_BLOCK = (2, 256)
ROW_TILE = 512
INPROJ_ROW_TILE = 512
COL_TILE = 1024

F32 = jnp.float32
BF16 = jnp.bfloat16


def _dot(a, b):
    return jnp.dot(a.astype(BF16), b.astype(BF16), preferred_element_type=F32)


def _dot_nt(a, b):
    return lax.dot_general(a.astype(BF16), b.astype(BF16), (((1,), (1,)), ((), ())),
                           preferred_element_type=F32)


def _dot_tn(a, b):
    return lax.dot_general(a.astype(BF16), b.astype(BF16), (((0,), (0,)), ((), ())),
                           preferred_element_type=F32)


def _dot_exact_lhs(a_bf16, x):
    x1 = x.astype(BF16)
    x2 = (x - x1.astype(F32)).astype(BF16)
    d = functools.partial(jnp.dot, preferred_element_type=F32)
    return d(a_bf16, x1) + d(a_bf16, x2)


def _sigmoid(x):
    return 1.0 / (1.0 + jnp.exp(-x))


def _silu(x):
    return x * _sigmoid(x)


def _pad_rows(x):
    return jnp.concatenate([x, jnp.zeros_like(x)], axis=0)


def _head_norm_gate(o, nw, gate):
    return o * lax.rsqrt(jnp.mean(o * o, axis=-1, keepdims=True) + NORM_EPS) * nw * gate


_Q, _K, _V, _Z, _HQ, _HF, _HI, _HG = range(8)


def _inproj_kernel(x_ref, nw_ref, w_ref, wab_ref, cw_ref, lb_ref, o_ref, logf_ref, oab_ref,
                   n_scr, xs_scr, halo_scr, *, tm, seq, layer):
    i = pl.program_id(0)
    j = pl.program_id(1)
    tn = o_ref.shape[1]

    @pl.when(j == 0)
    def _():
        x = x_ref[...]
        n = x * lax.rsqrt(jnp.mean(x * x, axis=-1, keepdims=True) + NORM_EPS) * nw_ref[...]
        nb = n.astype(BF16)
        n_scr[...] = nb
        oab_ref[...] = jnp.dot(nb, wab_ref[...], preferred_element_type=F32)

    acc = jnp.dot(n_scr[...], w_ref[...], preferred_element_type=F32)

    @pl.when(j <= _V)
    def _():
        xs_scr[8:8 + tm, :] = acc
        seq_start = (i * tm) % seq == 0

        @pl.when(seq_start)
        def _():
            xs_scr[0:8, :] = jnp.zeros((8, tn), F32)

        @pl.when(jnp.logical_not(seq_start))
        def _():
            xs_scr[0:8, :] = halo_scr[j]

        w = cw_ref[...]
        y = acc * w[CONV_WIDTH - 1:CONV_WIDTH, :]
        for t in range(CONV_WIDTH - 1):
            off = 8 - (CONV_WIDTH - 1) + t
            y = y + xs_scr[off:off + tm, :] * w[t:t + 1, :]
        halo_scr[j] = xs_scr[tm:tm + 8, :]
        y = _silu(y)

        @pl.when(j == _V)
        def _():
            o_ref[...] = y.astype(BF16)

        @pl.when(j < _V)
        def _():
            scale = jnp.where(j == _Q, HEAD_DIM ** -0.5, 1.0)
            for h in range(tn // HEAD_DIM):
                hs = slice(h * HEAD_DIM, (h + 1) * HEAD_DIM)
                yh = y[:, hs]
                inv_norm = lax.rsqrt(jnp.sum(yh * yh, axis=-1, keepdims=True) + L2_EPS)
                o_ref[:, hs] = (yh * (inv_norm * scale)).astype(BF16)

    @pl.when((j == _Z) | (j == _HQ) | (j == _HG))
    def _():
        o_ref[...] = _silu(acc).astype(BF16)

    @pl.when(j == _HI)
    def _():
        o_ref[...] = acc.astype(BF16)

    @pl.when(j == _HF)
    def _():
        logits = lb_ref[...]
        e = jnp.exp(logits - jnp.max(logits, axis=0, keepdims=True))
        lb = jnp.sum(e[:layer + 1], axis=0, keepdims=True) / jnp.sum(e, axis=0, keepdims=True)
        sig = _sigmoid(acc)
        o_ref[...] = ((1.0 - lb) * (1.0 - sig)).astype(BF16)
        logf_ref[...] = jnp.log(lb + (1.0 - lb) * sig)


def _inproj(x2, norm_w, w_main, w_ab, conv_w, lb_logits, seq, layer, tm, tn):
    m, d = x2.shape
    n = w_main.shape[1]
    assert n == 8 * tn and seq % tm == 0
    return pl.pallas_call(
        functools.partial(_inproj_kernel, tm=tm, seq=seq, layer=layer),
        grid=(m // tm, n // tn),
        in_specs=[
            pl.BlockSpec((tm, d), lambda i, j: (i, 0)),
            pl.BlockSpec((1, d), lambda i, j: (0, 0)),
            pl.BlockSpec((d, tn), lambda i, j: (0, j)),
            pl.BlockSpec((d, HEAD_DIM), lambda i, j: (0, 0)),
            pl.BlockSpec((CONV_WIDTH, tn), lambda i, j: (0, jnp.minimum(j, _V))),
            pl.BlockSpec((lb_logits.shape[0], tn), lambda i, j: (0, 0)),
        ],
        out_specs=[
            pl.BlockSpec((tm, tn), lambda i, j: (i, j)),
            pl.BlockSpec((tm, tn), lambda i, j: (i, 0)),
            pl.BlockSpec((tm, HEAD_DIM), lambda i, j: (i, 0)),
        ],
        out_shape=[jax.ShapeDtypeStruct((m, n), BF16), jax.ShapeDtypeStruct((m, tn), F32),
                   jax.ShapeDtypeStruct((m, HEAD_DIM), F32)],
        scratch_shapes=[pltpu.VMEM((tm, d), BF16), pltpu.VMEM((tm + 8, tn), F32),
                        pltpu.VMEM((_V + 1, 8, tn), F32)],
        compiler_params=pltpu.CompilerParams(
            dimension_semantics=("arbitrary", "arbitrary"), vmem_limit_bytes=VMEM_LIMIT_BYTES),
    )(x2, norm_w.reshape(1, d), w_main, w_ab, conv_w, lb_logits)


def _gdn_kernel(q_ref, k_ref, v_ref, z_ref, ab_ref, gp_ref, nw_ref, y_ref, s_scr, *, n_heads, hb, tb):
    h0 = pl.program_id(1) * hb

    @pl.when(pl.program_id(2) == 0)
    def _():
        s_scr[...] = jnp.zeros_like(s_scr)

    ab = ab_ref[...]
    gp = gp_ref[...]
    sp_in = ab + gp[1:2, :]
    g_tile = -jnp.exp(gp[0:1, :]) * (jnp.maximum(sp_in, 0.0) + jnp.log1p(jnp.exp(-jnp.abs(sp_in))))
    beta_tile = _sigmoid(ab)
    lane_t = lax.broadcasted_iota(jnp.int32, ab.shape, 1)

    def head_column(tile, idx):
        return jnp.broadcast_to(jnp.sum(jnp.where(lane_t == idx, tile, 0.0), axis=-1, keepdims=True), tile.shape)

    g_heads = [head_column(g_tile, h0 + j) for j in range(hb)]
    beta_heads = [head_column(beta_tile, h0 + j + n_heads) for j in range(hb)]

    row = lax.broadcasted_iota(jnp.int32, (CHUNK, HEAD_DIM), 0)
    lane = lax.broadcasted_iota(jnp.int32, (CHUNK, HEAD_DIM), 1)
    col = lane & (CHUNK - 1)
    left = lane < CHUNK
    incl = row >= col
    strict = row > col
    blk16 = (row >> 4) == (col >> 4)
    blk32 = (row >> 5) == (col >> 5)
    r64 = lax.broadcasted_iota(jnp.int32, (CHUNK, CHUNK), 0)
    c64 = lax.broadcasted_iota(jnp.int32, (CHUNK, CHUNK), 1)
    tri_incl = (r64 >= c64).astype(BF16)
    eye_left = jnp.where(lane == row, 1.0, 0.0)
    zeros = jnp.zeros((CHUNK, HEAD_DIM), F32)
    nw = nw_ref[...]

    n_chunks = tb // CHUNK
    items = [(j, c) for c in range(n_chunks) for j in range(hb)]
    rs = [slice(c * CHUNK, (c + 1) * CHUNK) for _, c in items]
    hs = [slice(j * HEAD_DIM, (j + 1) * HEAD_DIM) for j, _ in items]
    betas = [beta_heads[j][r] for (j, _), r in zip(items, rs)]
    gs = [g_heads[j][r] for (j, _), r in zip(items, rs)]
    dgs = [_dot_exact_lhs(tri_incl, jnp.concatenate([jnp.where(strict, g, 0.0), g], axis=1)) for g in gs]
    gcums = [dg[:, HEAD_DIM:] for dg in dgs]
    decays = [jnp.where(incl, jnp.exp(dg[:, :HEAD_DIM]), 0.0) for dg in dgs]
    g_lasts = [g[CHUNK - 1:CHUNK, :] for g in gcums]
    e_gs = [jnp.exp(g) for g in gcums]
    sds = [jnp.exp(gl) for gl in g_lasts]

    qbs = [q_ref[r, h] for r, h in zip(rs, hs)]
    kbs = [k_ref[r, h] for r, h in zip(rs, hs)]
    qns = [q.astype(F32) for q in qbs]
    kns = [k.astype(F32) for k in kbs]
    qks = [_dot_nt(jnp.concatenate([q, k], axis=0), jnp.concatenate([k, k], axis=0)) for q, k in zip(qbs, kbs)]
    attns = [jnp.where(left, qk[:CHUNK] * d, 0.0) for qk, d in zip(qks, decays)]
    lms = [jnp.where(strict & (~left), b * qk[CHUNK:] * d, 0.0) for b, qk, d in zip(betas, qks, decays)]

    ts = [eye_left - jnp.where(blk16, lm, 0.0) for lm in lms]
    for _ in range(4):
        ts = [_dot(t, jnp.concatenate([zeros, t], axis=0)) + jnp.where(left, t, 0.0) for t in ts]
    invs = [jnp.where(left, t, 0.0) for t in ts]
    for sel in (blk32 & (~blk16), ~blk32):
        w1s = [_dot(jnp.where(sel, lm, 0.0), jnp.concatenate([zeros, inv], axis=0)) for lm, inv in zip(lms, invs)]
        invs = [inv - _dot(inv, _pad_rows(w1)) for inv, w1 in zip(invs, w1s)]

    sols = [_dot(inv, _pad_rows(jnp.concatenate([b * e_g * kn, b * v_ref[r, h].astype(F32)], axis=1)))
            for inv, b, e_g, kn, r, h in zip(invs, betas, e_gs, kns, rs, hs)]
    k_ends = [kn * jnp.exp(gl - g) for kn, gl, g in zip(kns, g_lasts, gcums)]
    kws = [_dot_tn(k_end, sol) for k_end, sol in zip(k_ends, sols)]
    aws = [_dot(attn, _pad_rows(sol)) for attn, sol in zip(attns, sols)]
    lhss = [jnp.concatenate([kw[:, :HEAD_DIM], qn * e_g - aw[:, :HEAD_DIM]], axis=0)
            for kw, qn, e_g, aw in zip(kws, qns, e_gs, aws)]

    states = [s_scr[j] for j in range(hb)]
    for i, (j, _) in enumerate(items):
        r = _dot(lhss[i], states[j])
        o = r[HEAD_DIM:] + aws[i][:, HEAD_DIM:]
        states[j] = states[j] * sds[i] - r[:HEAD_DIM] + kws[i][:, HEAD_DIM:]
        y_ref[rs[i], hs[i]] = _head_norm_gate(o, nw, z_ref[rs[i], hs[i]].astype(F32)).astype(y_ref.dtype)
    for j in range(hb):
        s_scr[j] = states[j]


def _gdn(act, ab, gate_par, norm_w, batch, seq, n_heads, hb, tb):
    m = act.shape[0]
    nt = seq // tb
    ng = n_heads // hb
    width = hb * HEAD_DIM
    tok = lambda part: pl.BlockSpec((tb, width), lambda b, g, t: (b * nt + t, part * ng + g))
    return pl.pallas_call(
        functools.partial(_gdn_kernel, n_heads=n_heads, hb=hb, tb=tb),
        grid=(batch, ng, nt),
        in_specs=[
            tok(_Q), tok(_K), tok(_V), tok(_Z),
            pl.BlockSpec((tb, HEAD_DIM), lambda b, g, t: (b * nt + t, 0)),
            pl.BlockSpec((8, HEAD_DIM), lambda b, g, t: (0, 0)),
            pl.BlockSpec((1, HEAD_DIM), lambda b, g, t: (0, 0)),
        ],
        out_specs=pl.BlockSpec((tb, width), lambda b, g, t: (b * nt + t, g)),
        out_shape=jax.ShapeDtypeStruct((m, n_heads * HEAD_DIM), BF16),
        scratch_shapes=[pltpu.VMEM((hb, HEAD_DIM, HEAD_DIM), F32)],
        compiler_params=pltpu.CompilerParams(
            dimension_semantics=("arbitrary", "arbitrary", "arbitrary"), vmem_limit_bytes=VMEM_LIMIT_BYTES),
    )(act, act, act, act, ab, gate_par, norm_w.reshape(1, HEAD_DIM))


def _hgrn_kernel(q_ref, k_ref, i_ref, g_ref, logf_ref, nw_ref, y_ref, st_scr, *, hb, tb):
    @pl.when(pl.program_id(2) == 0)
    def _():
        st_scr[...] = jnp.zeros_like(st_scr)

    row = lax.broadcasted_iota(jnp.int32, (CHUNK, HEAD_DIM), 0)
    lane = lax.broadcasted_iota(jnp.int32, (CHUNK, HEAD_DIM), 1)
    r64 = lax.broadcasted_iota(jnp.int32, (CHUNK, CHUNK), 0)
    c64 = lax.broadcasted_iota(jnp.int32, (CHUNK, CHUNK), 1)
    tri_incl = (r64 >= c64).astype(BF16)
    nw = nw_ref[...]

    n_chunks = tb // CHUNK
    items = [(j, c) for c in range(n_chunks) for j in range(hb)]
    ids = range(len(items))
    rs = [slice(c * CHUNK, (c + 1) * CHUNK) for _, c in items]
    hs = [slice(j * HEAD_DIM, (j + 1) * HEAD_DIM) for j, _ in items]
    keys = [k_ref[r, h].astype(F32) for r, h in zip(rs, hs)]
    qss = [q_ref[r, h].astype(F32) for r, h in zip(rs, hs)]
    vbs = [i_ref[r, h] for r, h in zip(rs, hs)]
    vs = [v.astype(F32) for v in vbs]
    bs = [_dot_exact_lhs(tri_incl, logf_ref[r, h]) for r, h in zip(rs, hs)]
    b_lasts = [b[CHUNK - 1:CHUNK, :] for b in bs]

    a_s = [jnp.zeros((CHUNK, HEAD_DIM), F32) for _ in ids]
    n = CHUNK // 2
    while n >= 1:
        upper = (row & n) != 0
        same = (row & -(2 * n)) == (lane & -(2 * n))
        offset = row & (2 * n - 1)
        for i in ids:
            b = bs[i]
            if 2 * n >= 8:
                b_ref = jnp.concatenate(
                    [jnp.broadcast_to(b[r0 + n - 1:r0 + n, :], (2 * n, HEAD_DIM)) for r0 in range(0, CHUNK, 2 * n)],
                    axis=0)
            else:
                b_ref = b
                for o in range(2 * n):
                    if o != n - 1:
                        b_ref = jnp.where(offset == o, pltpu.roll(b, (o - (n - 1)) % CHUNK, 0), b_ref)
            ex = jnp.exp(jnp.where(upper, b - b_ref, b_ref - b))
            q_rel = jnp.where(upper, qss[i] * ex, 0.0)
            k_rel = jnp.where(upper, 0.0, keys[i] * ex)
            a_s[i] = a_s[i] + jnp.where(same, _dot_nt(q_rel, _pad_rows(k_rel)), 0.0)
        n //= 2

    intras = [_dot(a, _pad_rows(vb)) + jnp.sum(qs * key, axis=-1, keepdims=True) * v
              for a, vb, v, qs, key in zip(a_s, vbs, vs, qss, keys)]
    kvs = [_dot_tn(vb, key * jnp.exp(bl - b)) for vb, key, bl, b in zip(vbs, keys, b_lasts, bs)]
    qis = [qs * jnp.exp(b) for qs, b in zip(qss, bs)]

    states = [st_scr[j] for j in range(hb)]
    for i, (j, _) in enumerate(items):
        o = _dot_nt(qis[i], states[j]) + intras[i]
        states[j] = states[j] * jnp.exp(b_lasts[i]) + kvs[i]
        y_ref[rs[i], hs[i]] = _head_norm_gate(o, nw, g_ref[rs[i], hs[i]].astype(F32)).astype(y_ref.dtype)
    for j in range(hb):
        st_scr[j] = states[j]


def _hgrn(act, logf, norm_w, batch, seq, n_heads, hb, tb):
    m = act.shape[0]
    nt = seq // tb
    ng = n_heads // hb
    width = hb * HEAD_DIM
    tok = lambda part: pl.BlockSpec((tb, width), lambda b, g, t: (b * nt + t, part * ng + g))
    return pl.pallas_call(
        functools.partial(_hgrn_kernel, hb=hb, tb=tb),
        grid=(batch, ng, nt),
        in_specs=[
            tok(_HQ), tok(_HF), tok(_HI), tok(_HG),
            pl.BlockSpec((tb, width), lambda b, g, t: (b * nt + t, g)),
            pl.BlockSpec((1, HEAD_DIM), lambda b, g, t: (0, 0)),
        ],
        out_specs=pl.BlockSpec((tb, width), lambda b, g, t: (b * nt + t, g)),
        out_shape=jax.ShapeDtypeStruct((m, n_heads * HEAD_DIM), BF16),
        scratch_shapes=[pltpu.VMEM((hb, HEAD_DIM, HEAD_DIM), F32)],
        compiler_params=pltpu.CompilerParams(
            dimension_semantics=("arbitrary", "arbitrary", "arbitrary"), vmem_limit_bytes=VMEM_LIMIT_BYTES),
    )(act, act, act, act, logf, norm_w.reshape(1, HEAD_DIM))


def _outproj_kernel(ya_ref, yb_ref, wa_ref, wb_ref, x_ref, o_ref):
    acc = jnp.dot(ya_ref[...], wa_ref[...], preferred_element_type=F32)
    acc = acc + jnp.dot(yb_ref[...], wb_ref[...], preferred_element_type=F32)
    o_ref[...] = x_ref[...] + acc


def _outproj(ya, yb, w_a, w_b, x2, tm):
    m, d = x2.shape
    return pl.pallas_call(
        _outproj_kernel,
        grid=(m // tm,),
        in_specs=[
            pl.BlockSpec((tm, ya.shape[1]), lambda i: (i, 0)),
            pl.BlockSpec((tm, yb.shape[1]), lambda i: (i, 0)),
            pl.BlockSpec(w_a.shape, lambda i: (0, 0)),
            pl.BlockSpec(w_b.shape, lambda i: (0, 0)),
            pl.BlockSpec((tm, d), lambda i: (i, 0)),
        ],
        out_specs=pl.BlockSpec((tm, d), lambda i: (i, 0)),
        out_shape=jax.ShapeDtypeStruct((m, d), F32),
        compiler_params=pltpu.CompilerParams(
            dimension_semantics=("arbitrary",), vmem_limit_bytes=VMEM_LIMIT_BYTES),
    )(ya, yb, w_a, w_b, x2)


def _mlp_kernel(h_ref, nw_ref, w1_ref, w2_ref, fw_ref, o_ref, n_scr, *, final_norm):
    f = pl.program_id(1)

    @pl.when(f == 0)
    def _():
        x = h_ref[...]
        n = x * lax.rsqrt(jnp.mean(x * x, axis=-1, keepdims=True) + NORM_EPS) * nw_ref[...]
        n_scr[...] = n.astype(BF16)
        o_ref[...] = x

    hid = jnp.dot(n_scr[...], w1_ref[...], preferred_element_type=F32)
    hid = jnp.square(jnp.maximum(hid, 0.0)).astype(BF16)
    o_ref[...] += jnp.dot(hid, w2_ref[...], preferred_element_type=F32)

    if final_norm:
        @pl.when(f == pl.num_programs(1) - 1)
        def _():
            y = o_ref[...]
            o_ref[...] = y * lax.rsqrt(jnp.mean(y * y, axis=-1, keepdims=True) + NORM_EPS) * fw_ref[...]


def _mlp(h, norm_w, w1, w2, final_w, final_norm, tm, tf):
    m, d = h.shape
    ff = w1.shape[1]
    return pl.pallas_call(
        functools.partial(_mlp_kernel, final_norm=final_norm),
        grid=(m // tm, ff // tf),
        in_specs=[
            pl.BlockSpec((tm, d), lambda i, f: (i, 0)),
            pl.BlockSpec((1, d), lambda i, f: (0, 0)),
            pl.BlockSpec((d, tf), lambda i, f: (0, f)),
            pl.BlockSpec((tf, d), lambda i, f: (f, 0)),
            pl.BlockSpec((1, d), lambda i, f: (0, 0)),
        ],
        out_specs=pl.BlockSpec((tm, d), lambda i, f: (i, 0)),
        out_shape=jax.ShapeDtypeStruct((m, d), F32),
        scratch_shapes=[pltpu.VMEM((tm, d), BF16)],
        compiler_params=pltpu.CompilerParams(
            dimension_semantics=("arbitrary", "arbitrary"), vmem_limit_bytes=VMEM_LIMIT_BYTES),
    )(h, norm_w.reshape(1, d), w1, w2, final_w.reshape(1, d))


def _tile(n, target):
    t = min(n, target)
    while n % t:
        t //= 2
    return t


def kernel(x, w_in, conv_w, gdn_a_log, gdn_dt_bias, gdn_norm_w, hgrn_lb_logits, hgrn_norm_w, w_out,
           norm_mix_w, norm_ffn_w, w_ff1, w_ff2, norm_final_w):
    batch, seq, d_model = x.shape
    depth = w_in.shape[0]
    gh = gdn_a_log.shape[1]
    hh = hgrn_lb_logits.shape[1] // HEAD_DIM
    gw, hw = gh * HEAD_DIM, hh * HEAD_DIM
    assert 2 * gh <= HEAD_DIM and seq % CHUNK == 0
    assert gw == hw and w_in.shape[2] == 4 * gw + 2 * gh + 4 * hw

    m = batch * seq
    tm = _tile(m, ROW_TILE)
    gdn_hb, gdn_tb = _tile(gh, GDN_BLOCK[0]), _tile(seq, GDN_BLOCK[1])
    hgrn_hb, hgrn_tb = _tile(hh, HGRN_BLOCK[0]), _tile(seq, HGRN_BLOCK[1])
    h = x.reshape(m, d_model)
    for layer in range(depth):
        wl = w_in[layer]
        o1 = 4 * gw
        o2 = o1 + 2 * gh
        w_main = jnp.concatenate([wl[:, :o1], wl[:, o2:]], axis=1).astype(BF16)
        w_ab = jnp.pad(wl[:, o1:o2], ((0, 0), (0, HEAD_DIM - 2 * gh))).astype(BF16)
        gate_par = jnp.zeros((8, HEAD_DIM), F32)
        gate_par = gate_par.at[0, :gh].set(gdn_a_log[layer]).at[1, :gh].set(gdn_dt_bias[layer])
        w_o = w_out[layer].astype(BF16)

        act, logf, ab = _inproj(h, norm_mix_w[layer], w_main, w_ab, conv_w[layer], hgrn_lb_logits, seq, layer,
                                _tile(seq, INPROJ_ROW_TILE), gw)
        y_a = _gdn(act, ab, gate_par, gdn_norm_w[layer], batch, seq, gh, gdn_hb, gdn_tb)
        y_b = _hgrn(act, logf, hgrn_norm_w[layer], batch, seq, hh, hgrn_hb, hgrn_tb)
        h = _outproj(y_a, y_b, w_o[:gw], w_o[gw:], h, tm)
        h = _mlp(h, norm_ffn_w[layer], w_ff1[layer].astype(BF16), w_ff2[layer].astype(BF16), norm_final_w,
                 layer == depth - 1, tm, _tile(w_ff1.shape[2], COL_TILE))
    return h.reshape(batch, seq, d_model)
```

```python
import functools

import jax
import jax.numpy as jnp
from jax import lax
from jax.experimental import pallas as pl
from jax.experimental.pallas import tpu as pltpu

HEAD_DIM = 128
CHUNK = 64
CONV_WIDTH = 4
NORM_EPS = 1e-6
L2_EPS = 1e-6
VMEM_LIMIT_BYTES = 56 * 1024 * 1024

GDN_BLOCK = (8, 128)
HGRN_BLOCK = (2, 256)
ROW_TILE = 512
INPROJ_ROW_TILE = 512
INPROJ_SUB_ROWS = 128
COL_TILE = 1024

F32 = jnp.float32
BF16 = jnp.bfloat16


def _dot(a, b):
    return jnp.dot(a.astype(BF16), b.astype(BF16), preferred_element_type=F32)


def _dot_nt(a, b):
    return lax.dot_general(a.astype(BF16), b.astype(BF16), (((1,), (1,)), ((), ())),
                           preferred_element_type=F32)


def _dot_tn(a, b):
    return lax.dot_general(a.astype(BF16), b.astype(BF16), (((0,), (0,)), ((), ())),
                           preferred_element_type=F32)


def _dot_exact_lhs(a_bf16, x):
    x1 = x.astype(BF16)
    x2 = (x - x1.astype(F32)).astype(BF16)
    d = functools.partial(jnp.dot, preferred_element_type=F32)
    return d(a_bf16, x1) + d(a_bf16, x2)


def _sigmoid(x):
    return 1.0 / (1.0 + jnp.exp(-x))


def _silu(x):
    return x * _sigmoid(x)


def _pad_rows(x):
    return jnp.concatenate([x, jnp.zeros_like(x)], axis=0)


def _head_norm_gate(o, nw, gate):
    return o * lax.rsqrt(jnp.mean(o * o, axis=-1, keepdims=True) + NORM_EPS) * nw * gate


_Q, _K, _V, _Z, _HQ, _HF, _HI, _HG = range(8)


def _inproj_kernel(x_ref, nw_ref, wg_ref, wh_ref, wab_ref, cw_ref, lb_ref, o_ref, logf_ref, oab_ref,
                   n_scr, xs_scr, halo_scr, *, tm, seq, layer):
    i = pl.program_id(0)
    j = pl.program_id(1)
    tn = o_ref.shape[1]

    @pl.when(j == 0)
    def _():
        x = x_ref[...]
        n = x * lax.rsqrt(jnp.mean(x * x, axis=-1, keepdims=True) + NORM_EPS) * nw_ref[...]
        nb = n.astype(BF16)
        n_scr[...] = nb
        oab_ref[...] = jnp.dot(nb, wab_ref[...], preferred_element_type=F32)

    n_sub = tm // INPROJ_SUB_ROWS

    def project(epilogue, w_ref):
        acc = None
        for r in range(n_sub + 1):
            prev = acc
            if r < n_sub:
                rows = slice(r * INPROJ_SUB_ROWS, (r + 1) * INPROJ_SUB_ROWS)
                acc = jnp.dot(n_scr[rows, :], w_ref[...], preferred_element_type=F32)
            if r >= 1:
                epilogue(slice((r - 1) * INPROJ_SUB_ROWS, r * INPROJ_SUB_ROWS), prev)

    def conv_role(normalise):
        seq_start = (i * tm) % seq == 0
        xs_scr[0:8, :] = jnp.where(seq_start, 0.0, halo_scr[j])
        w = cw_ref[...]
        scale = jnp.where(j == _Q, HEAD_DIM ** -0.5, 1.0)

        def epilogue(rows, acc):
            xs_scr[8 + rows.start:8 + rows.stop, :] = acc
            y = acc * w[CONV_WIDTH - 1:CONV_WIDTH, :]
            for t in range(CONV_WIDTH - 1):
                off = 8 - (CONV_WIDTH - 1) + t
                y = y + xs_scr[off + rows.start:off + rows.stop, :] * w[t:t + 1, :]
            y = _silu(y)
            if not normalise:
                o_ref[rows, :] = y.astype(BF16)
                return
            for h in range(tn // HEAD_DIM):
                hs = slice(h * HEAD_DIM, (h + 1) * HEAD_DIM)
                yh = y[:, hs]
                inv_norm = lax.rsqrt(jnp.sum(yh * yh, axis=-1, keepdims=True) + L2_EPS)
                o_ref[rows, hs] = (yh * (inv_norm * scale)).astype(BF16)

        project(epilogue, wg_ref)
        halo_scr[j] = xs_scr[tm:tm + 8, :]

    @pl.when(j < _V)
    def _():
        conv_role(True)

    @pl.when(j == _V)
    def _():
        conv_role(False)

    def silu_epilogue(rows, acc):
        o_ref[rows, :] = _silu(acc).astype(BF16)

    @pl.when(j == _Z)
    def _():
        project(silu_epilogue, wg_ref)

    @pl.when((j == _HQ) | (j == _HG))
    def _():
        project(silu_epilogue, wh_ref)

    @pl.when(j == _HI)
    def _():
        def epilogue(rows, acc):
            o_ref[rows, :] = acc.astype(BF16)
        project(epilogue, wh_ref)

    @pl.when(j == _HF)
    def _():
        logits = lb_ref[...]
        e = jnp.exp(logits - jnp.max(logits, axis=0, keepdims=True))
        lb = jnp.sum(e[:layer + 1], axis=0, keepdims=True) / jnp.sum(e, axis=0, keepdims=True)

        def epilogue(rows, acc):
            sig = _sigmoid(acc)
            o_ref[rows, :] = ((1.0 - lb) * (1.0 - sig)).astype(BF16)
            logf_ref[rows, :] = jnp.log(lb + (1.0 - lb) * sig)
        project(epilogue, wh_ref)


def _inproj(x2, norm_w, w_gdn, w_hgrn, w_ab, conv_w, lb_logits, seq, layer, tm, tn):
    m, d = x2.shape
    n = w_gdn.shape[1] + w_hgrn.shape[1]
    assert w_gdn.shape[1] == (_Z + 1) * tn and n == 8 * tn and seq % tm == 0
    return pl.pallas_call(
        functools.partial(_inproj_kernel, tm=tm, seq=seq, layer=layer),
        grid=(m // tm, n // tn),
        in_specs=[
            pl.BlockSpec((tm, d), lambda i, j: (i, 0)),
            pl.BlockSpec((1, d), lambda i, j: (0, 0)),
            pl.BlockSpec((d, tn), lambda i, j: (0, jnp.minimum(j, _Z))),
            pl.BlockSpec((d, tn), lambda i, j: (0, jnp.maximum(j, _HQ) - _HQ)),
            pl.BlockSpec((d, HEAD_DIM), lambda i, j: (0, 0)),
            pl.BlockSpec((CONV_WIDTH, tn), lambda i, j: (0, jnp.minimum(j, _V))),
            pl.BlockSpec((lb_logits.shape[0], tn), lambda i, j: (0, 0)),
        ],
        out_specs=[
            pl.BlockSpec((tm, tn), lambda i, j: (i, j)),
            pl.BlockSpec((tm, tn), lambda i, j: (i, 0)),
            pl.BlockSpec((tm, HEAD_DIM), lambda i, j: (i, 0)),
        ],
        out_shape=[jax.ShapeDtypeStruct((m, n), BF16), jax.ShapeDtypeStruct((m, tn), F32),
                   jax.ShapeDtypeStruct((m, HEAD_DIM), F32)],
        scratch_shapes=[pltpu.VMEM((tm, d), BF16), pltpu.VMEM((tm + 8, tn), F32),
                        pltpu.VMEM((_V + 1, 8, tn), F32)],
        compiler_params=pltpu.CompilerParams(
            dimension_semantics=("arbitrary", "arbitrary"), vmem_limit_bytes=VMEM_LIMIT_BYTES),
    )(x2, norm_w.reshape(1, d), w_gdn, w_hgrn, w_ab, conv_w, lb_logits)


def _gdn_kernel(q_ref, k_ref, v_ref, z_ref, ab_ref, gp_ref, nw_ref, y_ref, s_scr, *, n_heads, hb, tb):
    h0 = pl.program_id(1) * hb

    @pl.when(pl.program_id(2) == 0)
    def _():
        s_scr[...] = jnp.zeros_like(s_scr)

    ab = ab_ref[...]
    gp = gp_ref[...]
    sp_in = ab + gp[1:2, :]
    g_tile = -jnp.exp(gp[0:1, :]) * (jnp.maximum(sp_in, 0.0) + jnp.log1p(jnp.exp(-jnp.abs(sp_in))))
    beta_tile = _sigmoid(ab)
    lane_t = lax.broadcasted_iota(jnp.int32, ab.shape, 1)

    def head_column(tile, idx):
        return jnp.broadcast_to(jnp.sum(jnp.where(lane_t == idx, tile, 0.0), axis=-1, keepdims=True), tile.shape)

    g_heads = [head_column(g_tile, h0 + j) for j in range(hb)]
    beta_heads = [head_column(beta_tile, h0 + j + n_heads) for j in range(hb)]

    row = lax.broadcasted_iota(jnp.int32, (CHUNK, HEAD_DIM), 0)
    lane = lax.broadcasted_iota(jnp.int32, (CHUNK, HEAD_DIM), 1)
    col = lane & (CHUNK - 1)
    left = lane < CHUNK
    incl = row >= col
    strict = row > col
    blk16 = (row >> 4) == (col >> 4)
    blk32 = (row >> 5) == (col >> 5)
    r64 = lax.broadcasted_iota(jnp.int32, (CHUNK, CHUNK), 0)
    c64 = lax.broadcasted_iota(jnp.int32, (CHUNK, CHUNK), 1)
    tri_incl = (r64 >= c64).astype(BF16)
    eye_left = jnp.where(lane == row, 1.0, 0.0)
    zeros = jnp.zeros((CHUNK, HEAD_DIM), F32)
    nw = nw_ref[...]

    n_chunks = tb // CHUNK
    items = [(j, c) for c in range(n_chunks) for j in range(hb)]
    rs = [slice(c * CHUNK, (c + 1) * CHUNK) for _, c in items]
    hs = [slice(j * HEAD_DIM, (j + 1) * HEAD_DIM) for j, _ in items]
    betas = [beta_heads[j][r] for (j, _), r in zip(items, rs)]
    gs = [g_heads[j][r] for (j, _), r in zip(items, rs)]
    dgs = [_dot_exact_lhs(tri_incl, jnp.concatenate([jnp.where(strict, g, 0.0), g], axis=1)) for g in gs]
    gcums = [dg[:, HEAD_DIM:] for dg in dgs]
    decays = [jnp.where(incl, jnp.exp(dg[:, :HEAD_DIM]), 0.0) for dg in dgs]
    g_lasts = [g[CHUNK - 1:CHUNK, :] for g in gcums]
    e_gs = [jnp.exp(g) for g in gcums]
    sds = [jnp.exp(gl) for gl in g_lasts]

    qbs = [q_ref[r, h] for r, h in zip(rs, hs)]
    kbs = [k_ref[r, h] for r, h in zip(rs, hs)]
    qns = [q.astype(F32) for q in qbs]
    kns = [k.astype(F32) for k in kbs]
    qks = [_dot_nt(jnp.concatenate([q, k], axis=0), jnp.concatenate([k, k], axis=0)) for q, k in zip(qbs, kbs)]
    attns = [jnp.where(left, qk[:CHUNK] * d, 0.0) for qk, d in zip(qks, decays)]
    lms = [jnp.where(strict & (~left), b * qk[CHUNK:] * d, 0.0) for b, qk, d in zip(betas, qks, decays)]

    ts = [eye_left - jnp.where(blk16, lm, 0.0) for lm in lms]
    for _ in range(4):
        ts = [_dot(t, jnp.concatenate([zeros, t], axis=0)) + jnp.where(left, t, 0.0) for t in ts]
    invs = [jnp.where(left, t, 0.0) for t in ts]
    for sel in (blk32 & (~blk16), ~blk32):
        w1s = [_dot(jnp.where(sel, lm, 0.0), jnp.concatenate([zeros, inv], axis=0)) for lm, inv in zip(lms, invs)]
        invs = [inv - _dot(inv, _pad_rows(w1)) for inv, w1 in zip(invs, w1s)]

    sols = [_dot(inv, _pad_rows(jnp.concatenate([b * e_g * kn, b * v_ref[r, h].astype(F32)], axis=1)))
            for inv, b, e_g, kn, r, h in zip(invs, betas, e_gs, kns, rs, hs)]
    k_ends = [kn * jnp.exp(gl - g) for kn, gl, g in zip(kns, g_lasts, gcums)]
    kws = [_dot_tn(k_end, sol) for k_end, sol in zip(k_ends, sols)]
    aws = [_dot(attn, _pad_rows(sol)) for attn, sol in zip(attns, sols)]
    lhss = [jnp.concatenate([kw[:, :HEAD_DIM], qn * e_g - aw[:, :HEAD_DIM]], axis=0)
            for kw, qn, e_g, aw in zip(kws, qns, e_gs, aws)]

    states = [s_scr[j] for j in range(hb)]
    for i, (j, _) in enumerate(items):
        r = _dot(lhss[i], states[j])
        o = r[HEAD_DIM:] + aws[i][:, HEAD_DIM:]
        states[j] = states[j] * sds[i] - r[:HEAD_DIM] + kws[i][:, HEAD_DIM:]
        y_ref[rs[i], hs[i]] = _head_norm_gate(o, nw, z_ref[rs[i], hs[i]].astype(F32)).astype(y_ref.dtype)
    for j in range(hb):
        s_scr[j] = states[j]


def _gdn(act, ab, gate_par, norm_w, batch, seq, n_heads, hb, tb):
    m = act.shape[0]
    nt = seq // tb
    ng = n_heads // hb
    width = hb * HEAD_DIM
    tok = lambda part: pl.BlockSpec((tb, width), lambda b, g, t: (b * nt + t, part * ng + g))
    return pl.pallas_call(
        functools.partial(_gdn_kernel, n_heads=n_heads, hb=hb, tb=tb),
        grid=(batch, ng, nt),
        in_specs=[
            tok(_Q), tok(_K), tok(_V), tok(_Z),
            pl.BlockSpec((tb, HEAD_DIM), lambda b, g, t: (b * nt + t, 0)),
            pl.BlockSpec((8, HEAD_DIM), lambda b, g, t: (0, 0)),
            pl.BlockSpec((1, HEAD_DIM), lambda b, g, t: (0, 0)),
        ],
        out_specs=pl.BlockSpec((tb, width), lambda b, g, t: (b * nt + t, g)),
        out_shape=jax.ShapeDtypeStruct((m, n_heads * HEAD_DIM), BF16),
        scratch_shapes=[pltpu.VMEM((hb, HEAD_DIM, HEAD_DIM), F32)],
        compiler_params=pltpu.CompilerParams(
            dimension_semantics=("arbitrary", "arbitrary", "arbitrary"), vmem_limit_bytes=VMEM_LIMIT_BYTES),
    )(act, act, act, act, ab, gate_par, norm_w.reshape(1, HEAD_DIM))


def _hgrn_kernel(q_ref, k_ref, i_ref, g_ref, logf_ref, nw_ref, y_ref, st_scr, *, hb, tb):
    @pl.when(pl.program_id(2) == 0)
    def _():
        st_scr[...] = jnp.zeros_like(st_scr)

    row = lax.broadcasted_iota(jnp.int32, (CHUNK, HEAD_DIM), 0)
    lane = lax.broadcasted_iota(jnp.int32, (CHUNK, HEAD_DIM), 1)
    r64 = lax.broadcasted_iota(jnp.int32, (CHUNK, CHUNK), 0)
    c64 = lax.broadcasted_iota(jnp.int32, (CHUNK, CHUNK), 1)
    tri_incl = (r64 >= c64).astype(BF16)
    nw = nw_ref[...]

    n_chunks = tb // CHUNK
    items = [(j, c) for c in range(n_chunks) for j in range(hb)]
    ids = range(len(items))
    rs = [slice(c * CHUNK, (c + 1) * CHUNK) for _, c in items]
    hs = [slice(j * HEAD_DIM, (j + 1) * HEAD_DIM) for j, _ in items]
    keys = [k_ref[r, h].astype(F32) for r, h in zip(rs, hs)]
    qss = [q_ref[r, h].astype(F32) for r, h in zip(rs, hs)]
    vbs = [i_ref[r, h] for r, h in zip(rs, hs)]
    vs = [v.astype(F32) for v in vbs]
    bs = [_dot_exact_lhs(tri_incl, logf_ref[r, h]) for r, h in zip(rs, hs)]
    b_lasts = [b[CHUNK - 1:CHUNK, :] for b in bs]

    a_s = [jnp.zeros((CHUNK, HEAD_DIM), F32) for _ in ids]
    n = CHUNK // 2
    while n >= 1:
        upper = (row & n) != 0
        keep = ((row & -(2 * n)) == (lane & -(2 * n))) & upper & ((lane & n) == 0)
        offset = row & (2 * n - 1)
        for i in ids:
            b = bs[i]
            if 2 * n >= 8:
                b_ref = jnp.concatenate(
                    [jnp.broadcast_to(b[r0 + n - 1:r0 + n, :], (2 * n, HEAD_DIM)) for r0 in range(0, CHUNK, 2 * n)],
                    axis=0)
            else:
                b_ref = b
                for o in range(2 * n):
                    if o != n - 1:
                        b_ref = jnp.where(offset == o, pltpu.roll(b, (o - (n - 1)) % CHUNK, 0), b_ref)
            x = jnp.where(upper, qss[i], keys[i]) * jnp.exp(-jnp.abs(b - b_ref))
            a_s[i] = a_s[i] + jnp.where(keep, _dot_nt(x, _pad_rows(x)), 0.0)
        n //= 2

    intras = [_dot(a, _pad_rows(vb)) + jnp.sum(qs * key, axis=-1, keepdims=True) * v
              for a, vb, v, qs, key in zip(a_s, vbs, vs, qss, keys)]
    kvs = [_dot_tn(vb, key * jnp.exp(bl - b)) for vb, key, bl, b in zip(vbs, keys, b_lasts, bs)]
    qis = [qs * jnp.exp(b) for qs, b in zip(qss, bs)]

    states = [st_scr[j] for j in range(hb)]
    for i, (j, _) in enumerate(items):
        o = _dot_nt(qis[i], states[j]) + intras[i]
        states[j] = states[j] * jnp.exp(b_lasts[i]) + kvs[i]
        y_ref[rs[i], hs[i]] = _head_norm_gate(o, nw, g_ref[rs[i], hs[i]].astype(F32)).astype(y_ref.dtype)
    for j in range(hb):
        st_scr[j] = states[j]


def _hgrn(act, logf, norm_w, batch, seq, n_heads, hb, tb):
    m = act.shape[0]
    nt = seq // tb
    ng = n_heads // hb
    width = hb * HEAD_DIM
    tok = lambda part: pl.BlockSpec((tb, width), lambda b, g, t: (b * nt + t, part * ng + g))
    return pl.pallas_call(
        functools.partial(_hgrn_kernel, hb=hb, tb=tb),
        grid=(batch, ng, nt),
        in_specs=[
            tok(_HQ), tok(_HF), tok(_HI), tok(_HG),
            pl.BlockSpec((tb, width), lambda b, g, t: (b * nt + t, g)),
            pl.BlockSpec((1, HEAD_DIM), lambda b, g, t: (0, 0)),
        ],
        out_specs=pl.BlockSpec((tb, width), lambda b, g, t: (b * nt + t, g)),
        out_shape=jax.ShapeDtypeStruct((m, n_heads * HEAD_DIM), BF16),
        scratch_shapes=[pltpu.VMEM((hb, HEAD_DIM, HEAD_DIM), F32)],
        compiler_params=pltpu.CompilerParams(
            dimension_semantics=("arbitrary", "arbitrary", "arbitrary"), vmem_limit_bytes=VMEM_LIMIT_BYTES),
    )(act, act, act, act, logf, norm_w.reshape(1, HEAD_DIM))


def _outproj_kernel(ya_ref, yb_ref, wa_ref, wb_ref, x_ref, o_ref):
    acc = jnp.dot(ya_ref[...], wa_ref[...], preferred_element_type=F32)
    acc = acc + jnp.dot(yb_ref[...], wb_ref[...], preferred_element_type=F32)
    o_ref[...] = x_ref[...] + acc


def _outproj(ya, yb, w_a, w_b, x2, tm):
    m, d = x2.shape
    return pl.pallas_call(
        _outproj_kernel,
        grid=(m // tm,),
        in_specs=[
            pl.BlockSpec((tm, ya.shape[1]), lambda i: (i, 0)),
            pl.BlockSpec((tm, yb.shape[1]), lambda i: (i, 0)),
            pl.BlockSpec(w_a.shape, lambda i: (0, 0)),
            pl.BlockSpec(w_b.shape, lambda i: (0, 0)),
            pl.BlockSpec((tm, d), lambda i: (i, 0)),
        ],
        out_specs=pl.BlockSpec((tm, d), lambda i: (i, 0)),
        out_shape=jax.ShapeDtypeStruct((m, d), F32),
        compiler_params=pltpu.CompilerParams(
            dimension_semantics=("arbitrary",), vmem_limit_bytes=VMEM_LIMIT_BYTES),
    )(ya, yb, w_a, w_b, x2)


def _mlp_kernel(h_ref, nw_ref, w1_ref, w2_ref, fw_ref, o_ref, n_scr, *, final_norm):
    f = pl.program_id(1)

    @pl.when(f == 0)
    def _():
        x = h_ref[...]
        n = x * lax.rsqrt(jnp.mean(x * x, axis=-1, keepdims=True) + NORM_EPS) * nw_ref[...]
        n_scr[...] = n.astype(BF16)
        o_ref[...] = x

    hid = jnp.dot(n_scr[...], w1_ref[...], preferred_element_type=F32)
    hid = jnp.square(jnp.maximum(hid, 0.0)).astype(BF16)
    o_ref[...] += jnp.dot(hid, w2_ref[...], preferred_element_type=F32)

    if final_norm:
        @pl.when(f == pl.num_programs(1) - 1)
        def _():
            y = o_ref[...]
            o_ref[...] = y * lax.rsqrt(jnp.mean(y * y, axis=-1, keepdims=True) + NORM_EPS) * fw_ref[...]


def _mlp(h, norm_w, w1, w2, final_w, final_norm, tm, tf):
    m, d = h.shape
    ff = w1.shape[1]
    return pl.pallas_call(
        functools.partial(_mlp_kernel, final_norm=final_norm),
        grid=(m // tm, ff // tf),
        in_specs=[
            pl.BlockSpec((tm, d), lambda i, f: (i, 0)),
            pl.BlockSpec((1, d), lambda i, f: (0, 0)),
            pl.BlockSpec((d, tf), lambda i, f: (0, f)),
            pl.BlockSpec((tf, d), lambda i, f: (f, 0)),
            pl.BlockSpec((1, d), lambda i, f: (0, 0)),
        ],
        out_specs=pl.BlockSpec((tm, d), lambda i, f: (i, 0)),
        out_shape=jax.ShapeDtypeStruct((m, d), F32),
        scratch_shapes=[pltpu.VMEM((tm, d), BF16)],
        compiler_params=pltpu.CompilerParams(
            dimension_semantics=("arbitrary", "arbitrary"), vmem_limit_bytes=VMEM_LIMIT_BYTES),
    )(h, norm_w.reshape(1, d), w1, w2, final_w.reshape(1, d))


def _tile(n, target):
    t = min(n, target)
    while n % t:
        t //= 2
    return t


def kernel(x, w_in, conv_w, gdn_a_log, gdn_dt_bias, gdn_norm_w, hgrn_lb_logits, hgrn_norm_w, w_out,
           norm_mix_w, norm_ffn_w, w_ff1, w_ff2, norm_final_w):
    batch, seq, d_model = x.shape
    depth = w_in.shape[0]
    gh = gdn_a_log.shape[1]
    hh = hgrn_lb_logits.shape[1] // HEAD_DIM
    gw, hw = gh * HEAD_DIM, hh * HEAD_DIM
    assert 2 * gh <= HEAD_DIM and seq % CHUNK == 0
    assert gw == hw and w_in.shape[2] == 4 * gw + 2 * gh + 4 * hw

    m = batch * seq
    tm = _tile(m, ROW_TILE)
    gdn_hb, gdn_tb = _tile(gh, GDN_BLOCK[0]), _tile(seq, GDN_BLOCK[1])
    hgrn_hb, hgrn_tb = _tile(hh, HGRN_BLOCK[0]), _tile(seq, HGRN_BLOCK[1])
    h = x.reshape(m, d_model)
    for layer in range(depth):
        wl = w_in[layer]
        o1 = 4 * gw
        o2 = o1 + 2 * gh
        w_gdn = wl[:, :o1].astype(BF16)
        w_hgrn = wl[:, o2:].astype(BF16)
        w_ab = jnp.pad(wl[:, o1:o2], ((0, 0), (0, HEAD_DIM - 2 * gh))).astype(BF16)
        gate_par = jnp.zeros((8, HEAD_DIM), F32)
        gate_par = gate_par.at[0, :gh].set(gdn_a_log[layer]).at[1, :gh].set(gdn_dt_bias[layer])
        w_o = w_out[layer].astype(BF16)

        act, logf, ab = _inproj(h, norm_mix_w[layer], w_gdn, w_hgrn, w_ab, conv_w[layer], hgrn_lb_logits, seq,
                                layer, _tile(seq, INPROJ_ROW_TILE), gw)
        y_a = _gdn(act, ab, gate_par, gdn_norm_w[layer], batch, seq, gh, gdn_hb, gdn_tb)
        y_b = _hgrn(act, logf, hgrn_norm_w[layer], batch, seq, hh, hgrn_hb, hgrn_tb)
        h = _outproj(y_a, y_b, w_o[:gw], w_o[gw:], h, tm)
        h = _mlp(h, norm_ffn_w[layer], w_ff1[layer].astype(BF16), w_ff2[layer].astype(BF16), norm_final_w,
                 layer == depth - 1, tm, _tile(w_ff1.shape[2], COL_TILE))
    return h.reshape(batch, seq, d_model)
```

```python
import functools

import jax
import jax.numpy as jnp
from jax import lax
from jax.experimental import pallas as pl
from jax.experimental.pallas import tpu as pltpu

HEAD_DIM = 128
CHUNK = 64
CONV_WIDTH = 4
NORM_EPS = 1e-6
L2_EPS = 1e-6
VMEM_LIMIT_BYTES = 56 * 1024 * 1024

GDN_BLOCK = (8, 256)
HGRN_BLOCK = (8, 256)
ROW_TILE = 512
MLP_TILE = (1024, 1024)
MLP_VMEM_LIMIT_BYTES = 62 * 1024 * 1024
INPROJ_ROW_TILE = 512
INPROJ_SUB_COLS = 256

F32 = jnp.float32
BF16 = jnp.bfloat16


def _dot(a, b):
    return jnp.dot(a.astype(BF16), b.astype(BF16), preferred_element_type=F32)


def _dot_nt(a, b):
    return lax.dot_general(a.astype(BF16), b.astype(BF16), (((1,), (1,)), ((), ())),
                           preferred_element_type=F32)


def _dot_tn(a, b):
    return lax.dot_general(a.astype(BF16), b.astype(BF16), (((0,), (0,)), ((), ())),
                           preferred_element_type=F32)


def _dot_exact_lhs(a_bf16, x):
    x1 = x.astype(BF16)
    x2 = (x - x1.astype(F32)).astype(BF16)
    d = functools.partial(jnp.dot, preferred_element_type=F32)
    return d(a_bf16, x1) + d(a_bf16, x2)


def _sigmoid(x):
    return 1.0 / (1.0 + jnp.exp(-x))


def _silu(x):
    return x * _sigmoid(x)


def _pad_rows(x):
    return jnp.concatenate([x, jnp.zeros_like(x)], axis=0)


def _head_norm_gate(o, nw, gate):
    return o * lax.rsqrt(jnp.mean(o * o, axis=-1, keepdims=True) + NORM_EPS) * nw * gate


_Q, _K, _V, _Z, _HQ, _HF, _HI, _HG = range(8)


def _inproj_kernel(x_ref, nw_ref, wg_ref, wh_ref, wab_ref, cw_ref, lb_ref, o_ref, logf_ref, oab_ref,
                   n_scr, xs_scr, halo_scr, *, tm, seq, layer):
    i = pl.program_id(0)
    j = pl.program_id(1)
    tn = o_ref.shape[1]

    @pl.when(j == 0)
    def _():
        x = x_ref[...]
        n = x * lax.rsqrt(jnp.mean(x * x, axis=-1, keepdims=True) + NORM_EPS) * nw_ref[...]
        nb = n.astype(BF16)
        n_scr[...] = nb
        oab_ref[...] = jnp.dot(nb, wab_ref[...], preferred_element_type=F32)

    n_sub = tn // INPROJ_SUB_COLS

    def project(epilogue, w_ref):
        acc = None
        for c in range(n_sub + 1):
            prev = acc
            if c < n_sub:
                cols = slice(c * INPROJ_SUB_COLS, (c + 1) * INPROJ_SUB_COLS)
                acc = jnp.dot(n_scr[...], w_ref[:, cols], preferred_element_type=F32)
            if c >= 1:
                epilogue(slice((c - 1) * INPROJ_SUB_COLS, c * INPROJ_SUB_COLS), prev)

    def conv_role(normalise):
        seq_start = (i * tm) % seq == 0
        xs_scr[0:8, :] = jnp.where(seq_start, 0.0, halo_scr[j])
        scale = jnp.where(j == _Q, HEAD_DIM ** -0.5, 1.0)

        def epilogue(cols, acc):
            w = cw_ref[:, cols]
            xs_scr[8:8 + tm, cols] = acc
            y = acc * w[CONV_WIDTH - 1:CONV_WIDTH, :]
            for t in range(CONV_WIDTH - 1):
                off = 8 - (CONV_WIDTH - 1) + t
                y = y + xs_scr[off:off + tm, cols] * w[t:t + 1, :]
            y = _silu(y)
            if not normalise:
                o_ref[:, cols] = y.astype(BF16)
                return
            for h in range(INPROJ_SUB_COLS // HEAD_DIM):
                yh = y[:, h * HEAD_DIM:(h + 1) * HEAD_DIM]
                inv_norm = lax.rsqrt(jnp.sum(yh * yh, axis=-1, keepdims=True) + L2_EPS)
                hs = slice(cols.start + h * HEAD_DIM, cols.start + (h + 1) * HEAD_DIM)
                o_ref[:, hs] = (yh * (inv_norm * scale)).astype(BF16)

        project(epilogue, wg_ref)
        halo_scr[j] = xs_scr[tm:tm + 8, :]

    @pl.when(j < _V)
    def _():
        conv_role(True)

    @pl.when(j == _V)
    def _():
        conv_role(False)

    def silu_epilogue(cols, acc):
        o_ref[:, cols] = _silu(acc).astype(BF16)

    @pl.when(j == _Z)
    def _():
        project(silu_epilogue, wg_ref)

    @pl.when((j == _HQ) | (j == _HG))
    def _():
        project(silu_epilogue, wh_ref)

    @pl.when(j == _HI)
    def _():
        def epilogue(cols, acc):
            o_ref[:, cols] = acc.astype(BF16)
        project(epilogue, wh_ref)

    @pl.when(j == _HF)
    def _():
        logits = lb_ref[...]
        e = jnp.exp(logits - jnp.max(logits, axis=0, keepdims=True))
        lb_all = jnp.sum(e[:layer + 1], axis=0, keepdims=True) / jnp.sum(e, axis=0, keepdims=True)

        def epilogue(cols, acc):
            lb = lb_all[:, cols]
            sig = _sigmoid(acc)
            o_ref[:, cols] = ((1.0 - lb) * (1.0 - sig)).astype(BF16)
            logf_ref[:, cols] = jnp.log(lb + (1.0 - lb) * sig)
        project(epilogue, wh_ref)


def _inproj(x2, norm_w, w_gdn, w_hgrn, w_ab, conv_w, lb_logits, seq, layer, tm, tn):
    m, d = x2.shape
    n = (w_gdn.shape[0] + w_hgrn.shape[0]) * tn
    assert w_gdn.shape == (_Z + 1, d, tn) and w_hgrn.shape == (_HG - _Z, d, tn) and seq % tm == 0
    return pl.pallas_call(
        functools.partial(_inproj_kernel, tm=tm, seq=seq, layer=layer),
        grid=(m // tm, n // tn),
        in_specs=[
            pl.BlockSpec((tm, d), lambda i, j: (i, 0)),
            pl.BlockSpec((1, d), lambda i, j: (0, 0)),
            pl.BlockSpec((None, d, tn), lambda i, j: (jnp.minimum(j, _Z), 0, 0)),
            pl.BlockSpec((None, d, tn), lambda i, j: (jnp.maximum(j, _HQ) - _HQ, 0, 0)),
            pl.BlockSpec((d, HEAD_DIM), lambda i, j: (0, 0)),
            pl.BlockSpec((CONV_WIDTH, tn), lambda i, j: (0, jnp.minimum(j, _V))),
            pl.BlockSpec((lb_logits.shape[0], tn), lambda i, j: (0, 0)),
        ],
        out_specs=[
            pl.BlockSpec((tm, tn), lambda i, j: (i, j)),
            pl.BlockSpec((tm, tn), lambda i, j: (i, 0)),
            pl.BlockSpec((tm, HEAD_DIM), lambda i, j: (i, 0)),
        ],
        out_shape=[jax.ShapeDtypeStruct((m, n), BF16), jax.ShapeDtypeStruct((m, tn), F32),
                   jax.ShapeDtypeStruct((m, HEAD_DIM), F32)],
        scratch_shapes=[pltpu.VMEM((tm, d), BF16), pltpu.VMEM((tm + 8, tn), F32),
                        pltpu.VMEM((_V + 1, 8, tn), F32)],
        compiler_params=pltpu.CompilerParams(
            dimension_semantics=("arbitrary", "arbitrary"), vmem_limit_bytes=VMEM_LIMIT_BYTES),
    )(x2, norm_w.reshape(1, d), w_gdn, w_hgrn, w_ab, conv_w, lb_logits)


def _gdn_kernel(q_ref, k_ref, v_ref, z_ref, ab_ref, gp_ref, nw_ref, w1_ref, w2_ref, y_ref, w1b_ref, w2b_ref,
                s_scr, *, n_heads, hb, tb):
    h0 = pl.program_id(1) * hb
    w1b_ref[...] = w1_ref[...].astype(BF16)
    w2b_ref[...] = w2_ref[...].astype(BF16)

    @pl.when(pl.program_id(2) == 0)
    def _():
        s_scr[...] = jnp.zeros_like(s_scr)

    ab = ab_ref[...]
    gp = gp_ref[...]
    sp_in = ab + gp[1:2, :]
    g_tile = -jnp.exp(gp[0:1, :]) * (jnp.maximum(sp_in, 0.0) + jnp.log1p(jnp.exp(-jnp.abs(sp_in))))
    beta_tile = _sigmoid(ab)
    lane_t = lax.broadcasted_iota(jnp.int32, ab.shape, 1)

    def head_column(tile, idx):
        return jnp.broadcast_to(jnp.sum(jnp.where(lane_t == idx, tile, 0.0), axis=-1, keepdims=True), tile.shape)

    g_heads = [head_column(g_tile, h0 + j) for j in range(hb)]
    beta_heads = [head_column(beta_tile, h0 + j + n_heads) for j in range(hb)]

    row = lax.broadcasted_iota(jnp.int32, (CHUNK, HEAD_DIM), 0)
    lane = lax.broadcasted_iota(jnp.int32, (CHUNK, HEAD_DIM), 1)
    col = lane & (CHUNK - 1)
    left = lane < CHUNK
    incl = row >= col
    strict = row > col
    blk16 = (row >> 4) == (col >> 4)
    blk32 = (row >> 5) == (col >> 5)
    r64 = lax.broadcasted_iota(jnp.int32, (CHUNK, CHUNK), 0)
    c64 = lax.broadcasted_iota(jnp.int32, (CHUNK, CHUNK), 1)
    tri_incl = (r64 >= c64).astype(BF16)
    eye_left = jnp.where(lane == row, 1.0, 0.0)
    zeros = jnp.zeros((CHUNK, HEAD_DIM), F32)
    nw = nw_ref[...]

    n_chunks = tb // CHUNK
    items = [(j, c) for c in range(n_chunks) for j in range(hb)]
    rs = [slice(c * CHUNK, (c + 1) * CHUNK) for _, c in items]
    hs = [slice(j * HEAD_DIM, (j + 1) * HEAD_DIM) for j, _ in items]
    betas = [beta_heads[j][r] for (j, _), r in zip(items, rs)]
    gs = [g_heads[j][r] for (j, _), r in zip(items, rs)]
    dgs = [_dot_exact_lhs(tri_incl, jnp.concatenate([jnp.where(strict, g, 0.0), g], axis=1)) for g in gs]
    gcums = [dg[:, HEAD_DIM:] for dg in dgs]
    decays = [jnp.where(incl, jnp.exp(dg[:, :HEAD_DIM]), 0.0) for dg in dgs]
    g_lasts = [g[CHUNK - 1:CHUNK, :] for g in gcums]
    e_gs = [jnp.exp(g) for g in gcums]
    sds = [jnp.exp(gl) for gl in g_lasts]

    qbs = [q_ref[r, h] for r, h in zip(rs, hs)]
    kbs = [k_ref[r, h] for r, h in zip(rs, hs)]
    qns = [q.astype(F32) for q in qbs]
    kns = [k.astype(F32) for k in kbs]
    qks = [_dot_nt(jnp.concatenate([q, k], axis=0), jnp.concatenate([k, k], axis=0)) for q, k in zip(qbs, kbs)]
    attns = [jnp.where(left, qk[:CHUNK] * d, 0.0) for qk, d in zip(qks, decays)]
    lms = [jnp.where(strict & (~left), b * qk[CHUNK:] * d, 0.0) for b, qk, d in zip(betas, qks, decays)]

    ts = [eye_left - jnp.where(blk16, lm, 0.0) for lm in lms]
    for _ in range(4):
        ts = [_dot(t, jnp.concatenate([zeros, t], axis=0)) + jnp.where(left, t, 0.0) for t in ts]
    invs = [jnp.where(left, t, 0.0) for t in ts]
    for sel in (blk32 & (~blk16), ~blk32):
        w1s = [_dot(jnp.where(sel, lm, 0.0), jnp.concatenate([zeros, inv], axis=0)) for lm, inv in zip(lms, invs)]
        invs = [inv - _dot(inv, _pad_rows(w1)) for inv, w1 in zip(invs, w1s)]

    sols = [_dot(inv, _pad_rows(jnp.concatenate([b * e_g * kn, b * v_ref[r, h].astype(F32)], axis=1)))
            for inv, b, e_g, kn, r, h in zip(invs, betas, e_gs, kns, rs, hs)]
    k_ends = [kn * jnp.exp(gl - g) for kn, gl, g in zip(kns, g_lasts, gcums)]
    kws = [_dot_tn(k_end, sol) for k_end, sol in zip(k_ends, sols)]
    aws = [_dot(attn, _pad_rows(sol)) for attn, sol in zip(attns, sols)]
    lhss = [jnp.concatenate([kw[:, :HEAD_DIM], qn * e_g - aw[:, :HEAD_DIM]], axis=0)
            for kw, qn, e_g, aw in zip(kws, qns, e_gs, aws)]

    states = [s_scr[j] for j in range(hb)]
    for i, (j, _) in enumerate(items):
        r = _dot(lhss[i], states[j])
        o = r[HEAD_DIM:] + aws[i][:, HEAD_DIM:]
        states[j] = states[j] * sds[i] - r[:HEAD_DIM] + kws[i][:, HEAD_DIM:]
        y_ref[rs[i], hs[i]] = _head_norm_gate(o, nw, z_ref[rs[i], hs[i]].astype(F32)).astype(y_ref.dtype)
    for j in range(hb):
        s_scr[j] = states[j]


def _gdn(act, ab, gate_par, norm_w, w1, w2, batch, seq, n_heads, hb, tb):
    m = act.shape[0]
    nt = seq // tb
    ng = n_heads // hb
    width = hb * HEAD_DIM
    steps = batch * ng * nt
    r1, r2 = w1.shape[0] // steps, w2.shape[0] // steps
    assert r1 * steps == w1.shape[0] and r2 * steps == w2.shape[0] and r1 % 16 == 0 and r2 % 16 == 0
    tok = lambda part: pl.BlockSpec((tb, width), lambda b, g, t: (b * nt + t, part * ng + g))
    slab = lambda rows, cols: pl.BlockSpec((rows, cols), lambda b, g, t: ((b * ng + g) * nt + t, 0))
    return pl.pallas_call(
        functools.partial(_gdn_kernel, n_heads=n_heads, hb=hb, tb=tb),
        grid=(batch, ng, nt),
        in_specs=[
            tok(_Q), tok(_K), tok(_V), tok(_Z),
            pl.BlockSpec((tb, HEAD_DIM), lambda b, g, t: (b * nt + t, 0)),
            pl.BlockSpec((8, HEAD_DIM), lambda b, g, t: (0, 0)),
            pl.BlockSpec((1, HEAD_DIM), lambda b, g, t: (0, 0)),
            slab(r1, w1.shape[1]), slab(r2, w2.shape[1]),
        ],
        out_specs=[pl.BlockSpec((tb, width), lambda b, g, t: (b * nt + t, g)),
                   slab(r1, w1.shape[1]), slab(r2, w2.shape[1])],
        out_shape=[jax.ShapeDtypeStruct((m, n_heads * HEAD_DIM), BF16),
                   jax.ShapeDtypeStruct(w1.shape, BF16), jax.ShapeDtypeStruct(w2.shape, BF16)],
        scratch_shapes=[pltpu.VMEM((hb, HEAD_DIM, HEAD_DIM), F32)],
        compiler_params=pltpu.CompilerParams(
            dimension_semantics=("arbitrary", "arbitrary", "arbitrary"), vmem_limit_bytes=VMEM_LIMIT_BYTES),
    )(act, act, act, act, ab, gate_par, norm_w.reshape(1, HEAD_DIM), w1, w2)


def _hgrn_kernel(q_ref, k_ref, i_ref, g_ref, logf_ref, nw_ref, y_ref, st_scr, *, hb, tb):
    @pl.when(pl.program_id(2) == 0)
    def _():
        st_scr[...] = jnp.zeros_like(st_scr)

    row = lax.broadcasted_iota(jnp.int32, (CHUNK, HEAD_DIM), 0)
    lane = lax.broadcasted_iota(jnp.int32, (CHUNK, HEAD_DIM), 1)
    r64 = lax.broadcasted_iota(jnp.int32, (CHUNK, CHUNK), 0)
    c64 = lax.broadcasted_iota(jnp.int32, (CHUNK, CHUNK), 1)
    tri_incl = (r64 >= c64).astype(BF16)
    nw = nw_ref[...]

    n_chunks = tb // CHUNK
    items = [(j, c) for c in range(n_chunks) for j in range(hb)]
    ids = range(len(items))
    rs = [slice(c * CHUNK, (c + 1) * CHUNK) for _, c in items]
    hs = [slice(j * HEAD_DIM, (j + 1) * HEAD_DIM) for j, _ in items]
    keys = [k_ref[r, h].astype(F32) for r, h in zip(rs, hs)]
    qss = [q_ref[r, h].astype(F32) for r, h in zip(rs, hs)]
    vbs = [i_ref[r, h] for r, h in zip(rs, hs)]
    vs = [v.astype(F32) for v in vbs]
    bs = [_dot_exact_lhs(tri_incl, logf_ref[r, h]) for r, h in zip(rs, hs)]
    b_lasts = [b[CHUNK - 1:CHUNK, :] for b in bs]

    a_s = [jnp.zeros((CHUNK, HEAD_DIM), F32) for _ in ids]
    n = CHUNK // 2
    while n >= 1:
        upper = (row & n) != 0
        keep = ((row & -(2 * n)) == (lane & -(2 * n))) & upper & ((lane & n) == 0)
        offset = row & (2 * n - 1)
        for i in ids:
            b = bs[i]
            if 2 * n >= 8:
                b_ref = jnp.concatenate(
                    [jnp.broadcast_to(b[r0 + n - 1:r0 + n, :], (2 * n, HEAD_DIM)) for r0 in range(0, CHUNK, 2 * n)],
                    axis=0)
            else:
                b_ref = b
                for o in range(2 * n):
                    if o != n - 1:
                        b_ref = jnp.where(offset == o, pltpu.roll(b, (o - (n - 1)) % CHUNK, 0), b_ref)
            x = jnp.where(upper, qss[i], keys[i]) * jnp.exp(-jnp.abs(b - b_ref))
            a_s[i] = a_s[i] + jnp.where(keep, _dot_nt(x, _pad_rows(x)), 0.0)
        n //= 2

    intras = [_dot(a, _pad_rows(vb)) + jnp.sum(qs * key, axis=-1, keepdims=True) * v
              for a, vb, v, qs, key in zip(a_s, vbs, vs, qss, keys)]
    kvs = [_dot_tn(vb, key * jnp.exp(bl - b)) for vb, key, bl, b in zip(vbs, keys, b_lasts, bs)]
    qis = [qs * jnp.exp(b) for qs, b in zip(qss, bs)]

    states = [st_scr[j] for j in range(hb)]
    for i, (j, _) in enumerate(items):
        o = _dot_nt(qis[i], states[j]) + intras[i]
        states[j] = states[j] * jnp.exp(b_lasts[i]) + kvs[i]
        y_ref[rs[i], hs[i]] = _head_norm_gate(o, nw, g_ref[rs[i], hs[i]].astype(F32)).astype(y_ref.dtype)
    for j in range(hb):
        st_scr[j] = states[j]


def _hgrn(act, logf, norm_w, batch, seq, n_heads, hb, tb):
    m = act.shape[0]
    nt = seq // tb
    ng = n_heads // hb
    width = hb * HEAD_DIM
    tok = lambda part: pl.BlockSpec((tb, width), lambda b, g, t: (b * nt + t, part * ng + g))
    return pl.pallas_call(
        functools.partial(_hgrn_kernel, hb=hb, tb=tb),
        grid=(batch, ng, nt),
        in_specs=[
            tok(_HQ), tok(_HF), tok(_HI), tok(_HG),
            pl.BlockSpec((tb, width), lambda b, g, t: (b * nt + t, g)),
            pl.BlockSpec((1, HEAD_DIM), lambda b, g, t: (0, 0)),
        ],
        out_specs=pl.BlockSpec((tb, width), lambda b, g, t: (b * nt + t, g)),
        out_shape=jax.ShapeDtypeStruct((m, n_heads * HEAD_DIM), BF16),
        scratch_shapes=[pltpu.VMEM((hb, HEAD_DIM, HEAD_DIM), F32)],
        compiler_params=pltpu.CompilerParams(
            dimension_semantics=("arbitrary", "arbitrary", "arbitrary"), vmem_limit_bytes=VMEM_LIMIT_BYTES),
    )(act, act, act, act, logf, norm_w.reshape(1, HEAD_DIM))


def _outproj_kernel(ya_ref, yb_ref, wa_ref, wb_ref, x_ref, o_ref):
    acc = jnp.dot(ya_ref[...], wa_ref[...], preferred_element_type=F32)
    acc = acc + jnp.dot(yb_ref[...], wb_ref[...], preferred_element_type=F32)
    o_ref[...] = x_ref[...] + acc


def _outproj(ya, yb, w_a, w_b, x2, tm):
    m, d = x2.shape
    return pl.pallas_call(
        _outproj_kernel,
        grid=(m // tm,),
        in_specs=[
            pl.BlockSpec((tm, ya.shape[1]), lambda i: (i, 0)),
            pl.BlockSpec((tm, yb.shape[1]), lambda i: (i, 0)),
            pl.BlockSpec(w_a.shape, lambda i: (0, 0)),
            pl.BlockSpec(w_b.shape, lambda i: (0, 0)),
            pl.BlockSpec((tm, d), lambda i: (i, 0)),
        ],
        out_specs=pl.BlockSpec((tm, d), lambda i: (i, 0)),
        out_shape=jax.ShapeDtypeStruct((m, d), F32),
        compiler_params=pltpu.CompilerParams(
            dimension_semantics=("arbitrary",), vmem_limit_bytes=VMEM_LIMIT_BYTES),
    )(ya, yb, w_a, w_b, x2)


def _mlp_kernel(h_ref, nw_ref, w1_ref, w2_ref, fw_ref, o_ref, n_scr, *, final_norm):
    f = pl.program_id(1)

    @pl.when(f == 0)
    def _():
        x = h_ref[...]
        n = x * lax.rsqrt(jnp.mean(x * x, axis=-1, keepdims=True) + NORM_EPS) * nw_ref[...]
        n_scr[...] = n.astype(BF16)
        o_ref[...] = x

    hid = jnp.dot(n_scr[...], w1_ref[...], preferred_element_type=F32)
    hid = jnp.square(jnp.maximum(hid, 0.0)).astype(BF16)
    o_ref[...] += jnp.dot(hid, w2_ref[...], preferred_element_type=F32)

    if final_norm:
        @pl.when(f == pl.num_programs(1) - 1)
        def _():
            y = o_ref[...]
            o_ref[...] = y * lax.rsqrt(jnp.mean(y * y, axis=-1, keepdims=True) + NORM_EPS) * fw_ref[...]


def _mlp(h, norm_w, w1, w2, final_w, final_norm, tm, tf):
    m, d = h.shape
    ff = w1.shape[1]
    return pl.pallas_call(
        functools.partial(_mlp_kernel, final_norm=final_norm),
        grid=(m // tm, ff // tf),
        in_specs=[
            pl.BlockSpec((tm, d), lambda i, f: (i, 0)),
            pl.BlockSpec((1, d), lambda i, f: (0, 0)),
            pl.BlockSpec((d, tf), lambda i, f: (0, f)),
            pl.BlockSpec((tf, d), lambda i, f: (f, 0)),
            pl.BlockSpec((1, d), lambda i, f: (0, 0)),
        ],
        out_specs=pl.BlockSpec((tm, d), lambda i, f: (i, 0)),
        out_shape=jax.ShapeDtypeStruct((m, d), F32),
        scratch_shapes=[pltpu.VMEM((tm, d), BF16)],
        compiler_params=pltpu.CompilerParams(
            dimension_semantics=("arbitrary", "arbitrary"), vmem_limit_bytes=MLP_VMEM_LIMIT_BYTES),
    )(h, norm_w.reshape(1, d), w1, w2, final_w.reshape(1, d))


def _tile(n, target):
    t = min(n, target)
    while n % t:
        t //= 2
    return t


def kernel(x, w_in, conv_w, gdn_a_log, gdn_dt_bias, gdn_norm_w, hgrn_lb_logits, hgrn_norm_w, w_out,
           norm_mix_w, norm_ffn_w, w_ff1, w_ff2, norm_final_w):
    batch, seq, d_model = x.shape
    depth = w_in.shape[0]
    gh = gdn_a_log.shape[1]
    hh = hgrn_lb_logits.shape[1] // HEAD_DIM
    gw, hw = gh * HEAD_DIM, hh * HEAD_DIM
    assert 2 * gh <= HEAD_DIM and seq % CHUNK == 0
    assert gw == hw and w_in.shape[2] == 4 * gw + 2 * gh + 4 * hw

    m = batch * seq
    tm = _tile(m, ROW_TILE)
    gdn_hb, gdn_tb = _tile(gh, GDN_BLOCK[0]), _tile(seq, GDN_BLOCK[1])
    hgrn_hb, hgrn_tb = _tile(hh, HGRN_BLOCK[0]), _tile(seq, HGRN_BLOCK[1])
    h = x.reshape(m, d_model)
    for layer in range(depth):
        wl = w_in[layer]
        o1 = 4 * gw
        o2 = o1 + 2 * gh
        w_gdn = wl[:, :o1].astype(BF16).reshape(d_model, 4, gw).transpose(1, 0, 2)
        w_hgrn = wl[:, o2:].astype(BF16).reshape(d_model, 4, hw).transpose(1, 0, 2)
        w_ab = jnp.pad(wl[:, o1:o2], ((0, 0), (0, HEAD_DIM - 2 * gh))).astype(BF16)
        gate_par = jnp.zeros((8, HEAD_DIM), F32)
        gate_par = gate_par.at[0, :gh].set(gdn_a_log[layer]).at[1, :gh].set(gdn_dt_bias[layer])
        w_o = w_out[layer].astype(BF16)

        act, logf, ab = _inproj(h, norm_mix_w[layer], w_gdn, w_hgrn, w_ab, conv_w[layer], hgrn_lb_logits, seq,
                                layer, _tile(seq, INPROJ_ROW_TILE), gw)
        y_a, w1, w2 = _gdn(act, ab, gate_par, gdn_norm_w[layer], w_ff1[layer], w_ff2[layer], batch, seq, gh,
                           gdn_hb, gdn_tb)
        y_b = _hgrn(act, logf, hgrn_norm_w[layer], batch, seq, hh, hgrn_hb, hgrn_tb)
        h = _outproj(y_a, y_b, w_o[:gw], w_o[gw:], h, tm)
        h = _mlp(h, norm_ffn_w[layer], w1, w2, norm_final_w, layer == depth - 1, _tile(m, MLP_TILE[0]),
                 _tile(w1.shape[1], MLP_TILE[1]))
    return h.reshape(batch, seq, d_model)
```

```python
import functools

import jax
import jax.numpy as jnp
from jax import lax
from jax.experimental import pallas as pl
from jax.experimental.pallas import tpu as pltpu

HEAD_DIM = 128
CHUNK = 64
CONV_WIDTH = 4
NORM_EPS = 1e-6
L2_EPS = 1e-6
VMEM_LIMIT_BYTES = 56 * 1024 * 1024

GDN_BLOCK = (8, 512)
HGRN_BLOCK = (8, 256)
ROW_TILE = 512
MLP_TILE = (1024, 1024)
BIG_VMEM_LIMIT_BYTES = 62 * 1024 * 1024
INPROJ_ROW_TILE = 1024
INPROJ_SUB_COLS = 256

F32 = jnp.float32
BF16 = jnp.bfloat16


def _dot(a, b):
    return jnp.dot(a.astype(BF16), b.astype(BF16), preferred_element_type=F32)


def _dot_nt(a, b):
    return lax.dot_general(a.astype(BF16), b.astype(BF16), (((1,), (1,)), ((), ())),
                           preferred_element_type=F32)


def _dot_tn(a, b):
    return lax.dot_general(a.astype(BF16), b.astype(BF16), (((0,), (0,)), ((), ())),
                           preferred_element_type=F32)


def _dot_exact_lhs(a_bf16, x):
    x1 = x.astype(BF16)
    x2 = (x - x1.astype(F32)).astype(BF16)
    d = functools.partial(jnp.dot, preferred_element_type=F32)
    return d(a_bf16, x1) + d(a_bf16, x2)


def _sigmoid(x):
    return 1.0 / (1.0 + jnp.exp(-x))


def _silu(x):
    return x * _sigmoid(x)


def _pad_rows(x):
    return jnp.concatenate([x, jnp.zeros_like(x)], axis=0)


def _head_norm_gate(o, nw, gate):
    return o * lax.rsqrt(jnp.mean(o * o, axis=-1, keepdims=True) + NORM_EPS) * nw * gate


_Q, _K, _V, _Z, _HQ, _HF, _HI, _HG = range(8)


def _inproj_kernel(x_ref, nw_ref, wg_ref, wh_ref, wab_ref, cw_ref, lb_ref, o_ref, logf_ref, oab_ref,
                   n_scr, halo_scr, *, tm, seq, layer):
    i = pl.program_id(0)
    j = pl.program_id(1)
    tn = o_ref.shape[1]

    @pl.when(j == 0)
    def _():
        x = x_ref[...]
        n = x * lax.rsqrt(jnp.mean(x * x, axis=-1, keepdims=True) + NORM_EPS) * nw_ref[...]
        nb = n.astype(BF16)
        n_scr[...] = nb
        oab_ref[...] = jnp.dot(nb, wab_ref[...], preferred_element_type=F32)

    n_sub = tn // INPROJ_SUB_COLS

    def project(epilogue, w_ref):
        acc = None
        for c in range(n_sub + 1):
            prev = acc
            if c < n_sub:
                cols = slice(c * INPROJ_SUB_COLS, (c + 1) * INPROJ_SUB_COLS)
                acc = jnp.dot(n_scr[...], w_ref[:, cols], preferred_element_type=F32)
            if c >= 1:
                epilogue(slice((c - 1) * INPROJ_SUB_COLS, c * INPROJ_SUB_COLS), prev)

    def conv_role(normalise):
        seq_start = (i * tm) % seq == 0
        halo_all = jnp.where(seq_start, 0.0, halo_scr[j])
        scale = jnp.where(j == _Q, HEAD_DIM ** -0.5, 1.0)
        row8 = lax.broadcasted_iota(jnp.int32, (8, INPROJ_SUB_COLS), 0)

        def epilogue(cols, acc):
            w = cw_ref[:, cols]
            halo = halo_all[:, cols]
            y = acc * w[CONV_WIDTH - 1:CONV_WIDTH, :]
            for s in range(1, CONV_WIDTH):
                rolled = pltpu.roll(acc, s, 0)
                head = jnp.where(row8 < s, pltpu.roll(halo, s, 0), rolled[0:8])
                shifted = jnp.concatenate([head, rolled[8:]], axis=0)
                y = y + shifted * w[CONV_WIDTH - 1 - s:CONV_WIDTH - s, :]
            halo_scr[j, :, cols] = acc[tm - 8:tm]
            y = _silu(y)
            if not normalise:
                o_ref[:, cols] = y.astype(BF16)
                return
            for h in range(INPROJ_SUB_COLS // HEAD_DIM):
                yh = y[:, h * HEAD_DIM:(h + 1) * HEAD_DIM]
                inv_norm = lax.rsqrt(jnp.sum(yh * yh, axis=-1, keepdims=True) + L2_EPS)
                hs = slice(cols.start + h * HEAD_DIM, cols.start + (h + 1) * HEAD_DIM)
                o_ref[:, hs] = (yh * (inv_norm * scale)).astype(BF16)

        project(epilogue, wg_ref)

    @pl.when(j < _V)
    def _():
        conv_role(True)

    @pl.when(j == _V)
    def _():
        conv_role(False)

    def silu_epilogue(cols, acc):
        o_ref[:, cols] = _silu(acc).astype(BF16)

    @pl.when(j == _Z)
    def _():
        project(silu_epilogue, wg_ref)

    @pl.when((j == _HQ) | (j == _HG))
    def _():
        project(silu_epilogue, wh_ref)

    @pl.when(j == _HI)
    def _():
        def epilogue(cols, acc):
            o_ref[:, cols] = acc.astype(BF16)
        project(epilogue, wh_ref)

    @pl.when(j == _HF)
    def _():
        logits = lb_ref[...]
        e = jnp.exp(logits - jnp.max(logits, axis=0, keepdims=True))
        lb_all = jnp.sum(e[:layer + 1], axis=0, keepdims=True) / jnp.sum(e, axis=0, keepdims=True)

        def epilogue(cols, acc):
            lb = lb_all[:, cols]
            sig = _sigmoid(acc)
            o_ref[:, cols] = ((1.0 - lb) * (1.0 - sig)).astype(BF16)
            logf_ref[:, cols] = jnp.log(lb + (1.0 - lb) * sig)
        project(epilogue, wh_ref)


def _inproj(x2, norm_w, w_gdn, w_hgrn, w_ab, conv_w, lb_logits, seq, layer, tm, tn):
    m, d = x2.shape
    n = w_gdn.shape[1] + w_hgrn.shape[1]
    assert w_gdn.shape[1] == (_Z + 1) * tn and n == 8 * tn and seq % tm == 0
    return pl.pallas_call(
        functools.partial(_inproj_kernel, tm=tm, seq=seq, layer=layer),
        grid=(m // tm, n // tn),
        in_specs=[
            pl.BlockSpec((tm, d), lambda i, j: (i, 0)),
            pl.BlockSpec((1, d), lambda i, j: (0, 0)),
            pl.BlockSpec((d, tn), lambda i, j: (0, jnp.minimum(j, _Z))),
            pl.BlockSpec((d, tn), lambda i, j: (0, jnp.maximum(j, _HQ) - _HQ)),
            pl.BlockSpec((d, HEAD_DIM), lambda i, j: (0, 0)),
            pl.BlockSpec((CONV_WIDTH, tn), lambda i, j: (0, jnp.minimum(j, _V))),
            pl.BlockSpec((lb_logits.shape[0], tn), lambda i, j: (0, 0)),
        ],
        out_specs=[
            pl.BlockSpec((tm, tn), lambda i, j: (i, j)),
            pl.BlockSpec((tm, tn), lambda i, j: (i, 0)),
            pl.BlockSpec((tm, HEAD_DIM), lambda i, j: (i, 0)),
        ],
        out_shape=[jax.ShapeDtypeStruct((m, n), BF16), jax.ShapeDtypeStruct((m, tn), F32),
                   jax.ShapeDtypeStruct((m, HEAD_DIM), F32)],
        scratch_shapes=[pltpu.VMEM((tm, d), BF16), pltpu.VMEM((_V + 1, 8, tn), F32)],
        compiler_params=pltpu.CompilerParams(
            dimension_semantics=("arbitrary", "arbitrary"), vmem_limit_bytes=BIG_VMEM_LIMIT_BYTES),
    )(x2, norm_w.reshape(1, d), w_gdn, w_hgrn, w_ab, conv_w, lb_logits)


def _gdn_kernel(q_ref, k_ref, v_ref, z_ref, ab_ref, gp_ref, nw_ref, w1_ref, w2_ref, y_ref, w1b_ref, w2b_ref,
                s_scr, *, n_heads, hb, tb):
    h0 = pl.program_id(1) * hb
    w1b_ref[...] = w1_ref[...].astype(BF16)
    w2b_ref[...] = w2_ref[...].astype(BF16)

    @pl.when(pl.program_id(2) == 0)
    def _():
        s_scr[...] = jnp.zeros_like(s_scr)

    ab = ab_ref[...]
    gp = gp_ref[...]
    sp_in = ab + gp[1:2, :]
    g_tile = -jnp.exp(gp[0:1, :]) * (jnp.maximum(sp_in, 0.0) + jnp.log1p(jnp.exp(-jnp.abs(sp_in))))
    beta_tile = _sigmoid(ab)
    lane_t = lax.broadcasted_iota(jnp.int32, ab.shape, 1)

    def head_column(tile, idx):
        return jnp.broadcast_to(jnp.sum(jnp.where(lane_t == idx, tile, 0.0), axis=-1, keepdims=True), tile.shape)

    g_heads = [head_column(g_tile, h0 + j) for j in range(hb)]
    beta_heads = [head_column(beta_tile, h0 + j + n_heads) for j in range(hb)]

    row = lax.broadcasted_iota(jnp.int32, (CHUNK, HEAD_DIM), 0)
    lane = lax.broadcasted_iota(jnp.int32, (CHUNK, HEAD_DIM), 1)
    col = lane & (CHUNK - 1)
    left = lane < CHUNK
    incl = row >= col
    strict = row > col
    blk16 = (row >> 4) == (col >> 4)
    blk32 = (row >> 5) == (col >> 5)
    r64 = lax.broadcasted_iota(jnp.int32, (CHUNK, CHUNK), 0)
    c64 = lax.broadcasted_iota(jnp.int32, (CHUNK, CHUNK), 1)
    tri_incl = (r64 >= c64).astype(BF16)
    eye_left = jnp.where(lane == row, 1.0, 0.0)
    zeros = jnp.zeros((CHUNK, HEAD_DIM), F32)
    nw = nw_ref[...]

    n_chunks = tb // CHUNK
    items = [(j, c) for c in range(n_chunks) for j in range(hb)]
    rs = [slice(c * CHUNK, (c + 1) * CHUNK) for _, c in items]
    hs = [slice(j * HEAD_DIM, (j + 1) * HEAD_DIM) for j, _ in items]
    betas = [beta_heads[j][r] for (j, _), r in zip(items, rs)]
    gs = [g_heads[j][r] for (j, _), r in zip(items, rs)]
    dgs = [_dot_exact_lhs(tri_incl, jnp.concatenate([jnp.where(strict, g, 0.0), g], axis=1)) for g in gs]
    gcums = [dg[:, HEAD_DIM:] for dg in dgs]
    decays = [jnp.where(incl, jnp.exp(dg[:, :HEAD_DIM]), 0.0) for dg in dgs]
    g_lasts = [g[CHUNK - 1:CHUNK, :] for g in gcums]
    e_gs = [jnp.exp(g) for g in gcums]
    sds = [jnp.exp(gl) for gl in g_lasts]

    qbs = [q_ref[r, h] for r, h in zip(rs, hs)]
    kbs = [k_ref[r, h] for r, h in zip(rs, hs)]
    qns = [q.astype(F32) for q in qbs]
    kns = [k.astype(F32) for k in kbs]
    qks = [_dot_nt(jnp.concatenate([q, k], axis=0), jnp.concatenate([k, k], axis=0)) for q, k in zip(qbs, kbs)]
    attns = [jnp.where(left, qk[:CHUNK] * d, 0.0) for qk, d in zip(qks, decays)]
    lms = [jnp.where(strict & (~left), b * qk[CHUNK:] * d, 0.0) for b, qk, d in zip(betas, qks, decays)]

    ts = [eye_left - jnp.where(blk16, lm, 0.0) for lm in lms]
    for _ in range(4):
        ts = [_dot(t, jnp.concatenate([zeros, t], axis=0)) + jnp.where(left, t, 0.0) for t in ts]
    invs = [jnp.where(left, t, 0.0) for t in ts]
    for sel in (blk32 & (~blk16), ~blk32):
        w1s = [_dot(jnp.where(sel, lm, 0.0), jnp.concatenate([zeros, inv], axis=0)) for lm, inv in zip(lms, invs)]
        invs = [inv - _dot(inv, _pad_rows(w1)) for inv, w1 in zip(invs, w1s)]

    sols = [_dot(inv, _pad_rows(jnp.concatenate([b * e_g * kn, b * v_ref[r, h].astype(F32)], axis=1)))
            for inv, b, e_g, kn, r, h in zip(invs, betas, e_gs, kns, rs, hs)]
    k_ends = [kn * jnp.exp(gl - g) for kn, gl, g in zip(kns, g_lasts, gcums)]
    kws = [_dot_tn(k_end, sol) for k_end, sol in zip(k_ends, sols)]
    aws = [_dot(attn, _pad_rows(sol)) for attn, sol in zip(attns, sols)]
    lhss = [jnp.concatenate([kw[:, :HEAD_DIM], qn * e_g - aw[:, :HEAD_DIM]], axis=0)
            for kw, qn, e_g, aw in zip(kws, qns, e_gs, aws)]

    states = [s_scr[j] for j in range(hb)]
    for i, (j, _) in enumerate(items):
        r = _dot(lhss[i], states[j])
        o = r[HEAD_DIM:] + aws[i][:, HEAD_DIM:]
        states[j] = states[j] * sds[i] - r[:HEAD_DIM] + kws[i][:, HEAD_DIM:]
        y_ref[rs[i], hs[i]] = _head_norm_gate(o, nw, z_ref[rs[i], hs[i]].astype(F32)).astype(y_ref.dtype)
    for j in range(hb):
        s_scr[j] = states[j]


def _gdn(act, ab, gate_par, norm_w, w1, w2, batch, seq, n_heads, hb, tb):
    m = act.shape[0]
    nt = seq // tb
    ng = n_heads // hb
    width = hb * HEAD_DIM
    steps = batch * ng * nt
    r1, r2 = w1.shape[0] // steps, w2.shape[0] // steps
    assert r1 * steps == w1.shape[0] and r2 * steps == w2.shape[0] and r1 % 16 == 0 and r2 % 16 == 0
    tok = lambda part: pl.BlockSpec((tb, width), lambda b, g, t: (b * nt + t, part * ng + g))
    slab = lambda rows, cols: pl.BlockSpec((rows, cols), lambda b, g, t: ((b * ng + g) * nt + t, 0))
    return pl.pallas_call(
        functools.partial(_gdn_kernel, n_heads=n_heads, hb=hb, tb=tb),
        grid=(batch, ng, nt),
        in_specs=[
            tok(_Q), tok(_K), tok(_V), tok(_Z),
            pl.BlockSpec((tb, HEAD_DIM), lambda b, g, t: (b * nt + t, 0)),
            pl.BlockSpec((8, HEAD_DIM), lambda b, g, t: (0, 0)),
            pl.BlockSpec((1, HEAD_DIM), lambda b, g, t: (0, 0)),
            slab(r1, w1.shape[1]), slab(r2, w2.shape[1]),
        ],
        out_specs=[pl.BlockSpec((tb, width), lambda b, g, t: (b * nt + t, g)),
                   slab(r1, w1.shape[1]), slab(r2, w2.shape[1])],
        out_shape=[jax.ShapeDtypeStruct((m, n_heads * HEAD_DIM), BF16),
                   jax.ShapeDtypeStruct(w1.shape, BF16), jax.ShapeDtypeStruct(w2.shape, BF16)],
        scratch_shapes=[pltpu.VMEM((hb, HEAD_DIM, HEAD_DIM), F32)],
        compiler_params=pltpu.CompilerParams(
            dimension_semantics=("arbitrary", "arbitrary", "arbitrary"), vmem_limit_bytes=VMEM_LIMIT_BYTES),
    )(act, act, act, act, ab, gate_par, norm_w.reshape(1, HEAD_DIM), w1, w2)


def _hgrn_kernel(q_ref, k_ref, i_ref, g_ref, logf_ref, nw_ref, y_ref, st_scr, *, hb, tb):
    @pl.when(pl.program_id(2) == 0)
    def _():
        st_scr[...] = jnp.zeros_like(st_scr)

    row = lax.broadcasted_iota(jnp.int32, (CHUNK, HEAD_DIM), 0)
    lane = lax.broadcasted_iota(jnp.int32, (CHUNK, HEAD_DIM), 1)
    r64 = lax.broadcasted_iota(jnp.int32, (CHUNK, CHUNK), 0)
    c64 = lax.broadcasted_iota(jnp.int32, (CHUNK, CHUNK), 1)
    tri_incl = (r64 >= c64).astype(BF16)
    nw = nw_ref[...]

    n_chunks = tb // CHUNK
    items = [(j, c) for c in range(n_chunks) for j in range(hb)]
    ids = range(len(items))
    rs = [slice(c * CHUNK, (c + 1) * CHUNK) for _, c in items]
    hs = [slice(j * HEAD_DIM, (j + 1) * HEAD_DIM) for j, _ in items]
    keys = [k_ref[r, h].astype(F32) for r, h in zip(rs, hs)]
    qss = [q_ref[r, h].astype(F32) for r, h in zip(rs, hs)]
    vbs = [i_ref[r, h] for r, h in zip(rs, hs)]
    vs = [v.astype(F32) for v in vbs]
    bs = [_dot_exact_lhs(tri_incl, logf_ref[r, h]) for r, h in zip(rs, hs)]
    b_lasts = [b[CHUNK - 1:CHUNK, :] for b in bs]

    a_s = [jnp.zeros((CHUNK, HEAD_DIM), F32) for _ in ids]
    n = CHUNK // 2
    while n >= 1:
        upper = (row & n) != 0
        keep = ((row & -(2 * n)) == (lane & -(2 * n))) & upper & ((lane & n) == 0)
        offset = row & (2 * n - 1)
        for i in ids:
            b = bs[i]
            if 2 * n >= 8:
                b_ref = jnp.concatenate(
                    [jnp.broadcast_to(b[r0 + n - 1:r0 + n, :], (2 * n, HEAD_DIM)) for r0 in range(0, CHUNK, 2 * n)],
                    axis=0)
            else:
                b_ref = b
                for o in range(2 * n):
                    if o != n - 1:
                        b_ref = jnp.where(offset == o, pltpu.roll(b, (o - (n - 1)) % CHUNK, 0), b_ref)
            x = jnp.where(upper, qss[i], keys[i]) * jnp.exp(-jnp.abs(b - b_ref))
            a_s[i] = a_s[i] + jnp.where(keep, _dot_nt(x, _pad_rows(x)), 0.0)
        n //= 2

    intras = [_dot(a, _pad_rows(vb)) + jnp.sum(qs * key, axis=-1, keepdims=True) * v
              for a, vb, v, qs, key in zip(a_s, vbs, vs, qss, keys)]
    kvs = [_dot_tn(vb, key * jnp.exp(bl - b)) for vb, key, bl, b in zip(vbs, keys, b_lasts, bs)]
    qis = [qs * jnp.exp(b) for qs, b in zip(qss, bs)]

    states = [st_scr[j] for j in range(hb)]
    for i, (j, _) in enumerate(items):
        o = _dot_nt(qis[i], states[j]) + intras[i]
        states[j] = states[j] * jnp.exp(b_lasts[i]) + kvs[i]
        y_ref[rs[i], hs[i]] = _head_norm_gate(o, nw, g_ref[rs[i], hs[i]].astype(F32)).astype(y_ref.dtype)
    for j in range(hb):
        st_scr[j] = states[j]


def _hgrn(act, logf, norm_w, batch, seq, n_heads, hb, tb):
    m = act.shape[0]
    nt = seq // tb
    ng = n_heads // hb
    width = hb * HEAD_DIM
    tok = lambda part: pl.BlockSpec((tb, width), lambda b, g, t: (b * nt + t, part * ng + g))
    return pl.pallas_call(
        functools.partial(_hgrn_kernel, hb=hb, tb=tb),
        grid=(batch, ng, nt),
        in_specs=[
            tok(_HQ), tok(_HF), tok(_HI), tok(_HG),
            pl.BlockSpec((tb, width), lambda b, g, t: (b * nt + t, g)),
            pl.BlockSpec((1, HEAD_DIM), lambda b, g, t: (0, 0)),
        ],
        out_specs=pl.BlockSpec((tb, width), lambda b, g, t: (b * nt + t, g)),
        out_shape=jax.ShapeDtypeStruct((m, n_heads * HEAD_DIM), BF16),
        scratch_shapes=[pltpu.VMEM((hb, HEAD_DIM, HEAD_DIM), F32)],
        compiler_params=pltpu.CompilerParams(
            dimension_semantics=("arbitrary", "arbitrary", "arbitrary"), vmem_limit_bytes=VMEM_LIMIT_BYTES),
    )(act, act, act, act, logf, norm_w.reshape(1, HEAD_DIM))


def _outproj_kernel(ya_ref, yb_ref, wa_ref, wb_ref, x_ref, o_ref):
    acc = jnp.dot(ya_ref[...], wa_ref[...], preferred_element_type=F32)
    acc = acc + jnp.dot(yb_ref[...], wb_ref[...], preferred_element_type=F32)
    o_ref[...] = x_ref[...] + acc


def _outproj(ya, yb, w_a, w_b, x2, tm):
    m, d = x2.shape
    return pl.pallas_call(
        _outproj_kernel,
        grid=(m // tm,),
        in_specs=[
            pl.BlockSpec((tm, ya.shape[1]), lambda i: (i, 0)),
            pl.BlockSpec((tm, yb.shape[1]), lambda i: (i, 0)),
            pl.BlockSpec(w_a.shape, lambda i: (0, 0)),
            pl.BlockSpec(w_b.shape, lambda i: (0, 0)),
            pl.BlockSpec((tm, d), lambda i: (i, 0)),
        ],
        out_specs=pl.BlockSpec((tm, d), lambda i: (i, 0)),
        out_shape=jax.ShapeDtypeStruct((m, d), F32),
        compiler_params=pltpu.CompilerParams(
            dimension_semantics=("arbitrary",), vmem_limit_bytes=VMEM_LIMIT_BYTES),
    )(ya, yb, w_a, w_b, x2)


def _mlp_kernel(h_ref, nw_ref, w1_ref, w2_ref, fw_ref, o_ref, n_scr, *, final_norm):
    f = pl.program_id(1)

    @pl.when(f == 0)
    def _():
        x = h_ref[...]
        n = x * lax.rsqrt(jnp.mean(x * x, axis=-1, keepdims=True) + NORM_EPS) * nw_ref[...]
        n_scr[...] = n.astype(BF16)
        o_ref[...] = x

    hid = jnp.dot(n_scr[...], w1_ref[...], preferred_element_type=F32)
    hid = jnp.square(jnp.maximum(hid, 0.0)).astype(BF16)
    o_ref[...] += jnp.dot(hid, w2_ref[...], preferred_element_type=F32)

    if final_norm:
        @pl.when(f == pl.num_programs(1) - 1)
        def _():
            y = o_ref[...]
            o_ref[...] = y * lax.rsqrt(jnp.mean(y * y, axis=-1, keepdims=True) + NORM_EPS) * fw_ref[...]


def _mlp(h, norm_w, w1, w2, final_w, final_norm, tm, tf):
    m, d = h.shape
    ff = w1.shape[1]
    return pl.pallas_call(
        functools.partial(_mlp_kernel, final_norm=final_norm),
        grid=(m // tm, ff // tf),
        in_specs=[
            pl.BlockSpec((tm, d), lambda i, f: (i, 0)),
            pl.BlockSpec((1, d), lambda i, f: (0, 0)),
            pl.BlockSpec((d, tf), lambda i, f: (0, f)),
            pl.BlockSpec((tf, d), lambda i, f: (f, 0)),
            pl.BlockSpec((1, d), lambda i, f: (0, 0)),
        ],
        out_specs=pl.BlockSpec((tm, d), lambda i, f: (i, 0)),
        out_shape=jax.ShapeDtypeStruct((m, d), F32),
        scratch_shapes=[pltpu.VMEM((tm, d), BF16)],
        compiler_params=pltpu.CompilerParams(
            dimension_semantics=("arbitrary", "arbitrary"), vmem_limit_bytes=BIG_VMEM_LIMIT_BYTES),
    )(h, norm_w.reshape(1, d), w1, w2, final_w.reshape(1, d))


def _tile(n, target):
    t = min(n, target)
    while n % t:
        t //= 2
    return t


def kernel(x, w_in, conv_w, gdn_a_log, gdn_dt_bias, gdn_norm_w, hgrn_lb_logits, hgrn_norm_w, w_out,
           norm_mix_w, norm_ffn_w, w_ff1, w_ff2, norm_final_w):
    batch, seq, d_model = x.shape
    depth = w_in.shape[0]
    gh = gdn_a_log.shape[1]
    hh = hgrn_lb_logits.shape[1] // HEAD_DIM
    gw, hw = gh * HEAD_DIM, hh * HEAD_DIM
    assert 2 * gh <= HEAD_DIM and seq % CHUNK == 0
    assert gw == hw and w_in.shape[2] == 4 * gw + 2 * gh + 4 * hw

    m = batch * seq
    tm = _tile(m, ROW_TILE)
    gdn_hb, gdn_tb = _tile(gh, GDN_BLOCK[0]), _tile(seq, GDN_BLOCK[1])
    hgrn_hb, hgrn_tb = _tile(hh, HGRN_BLOCK[0]), _tile(seq, HGRN_BLOCK[1])
    h = x.reshape(m, d_model)
    for layer in range(depth):
        wl = w_in[layer]
        o1 = 4 * gw
        o2 = o1 + 2 * gh
        w_gdn = wl[:, :o1].astype(BF16)
        w_hgrn = wl[:, o2:].astype(BF16)
        w_ab = jnp.pad(wl[:, o1:o2], ((0, 0), (0, HEAD_DIM - 2 * gh))).astype(BF16)
        gate_par = jnp.zeros((8, HEAD_DIM), F32)
        gate_par = gate_par.at[0, :gh].set(gdn_a_log[layer]).at[1, :gh].set(gdn_dt_bias[layer])
        w_o = w_out[layer].astype(BF16)

        act, logf, ab = _inproj(h, norm_mix_w[layer], w_gdn, w_hgrn, w_ab, conv_w[layer], hgrn_lb_logits, seq,
                                layer, _tile(seq, INPROJ_ROW_TILE), gw)
        y_a, w1, w2 = _gdn(act, ab, gate_par, gdn_norm_w[layer], w_ff1[layer], w_ff2[layer], batch, seq, gh,
                           gdn_hb, gdn_tb)
        y_b = _hgrn(act, logf, hgrn_norm_w[layer], batch, seq, hh, hgrn_hb, hgrn_tb)
        h = _outproj(y_a, y_b, w_o[:gw], w_o[gw:], h, tm)
        h = _mlp(h, norm_ffn_w[layer], w1, w2, norm_final_w, layer == depth - 1, _tile(m, MLP_TILE[0]),
                 _tile(w1.shape[1], MLP_TILE[1]))
    return h.reshape(batch, seq, d_model)
```

```python
import functools

import jax
import jax.numpy as jnp
from jax import lax
from jax.experimental import pallas as pl
from jax.experimental.pallas import tpu as pltpu

HEAD_DIM = 128
SUBLANES = 8
CHUNK = 64
CONV_WIDTH = 4
NORM_EPS = 1e-6
L2_EPS = 1e-6
VMEM_LIMIT_BYTES = 56 * 1024 * 1024

GDN_BLOCK = (8, 512)
HGRN_BLOCK = (8, 256)
ROW_TILE = 512
MLP_TILE = (1024, 1024)
BIG_VMEM_LIMIT_BYTES = 62 * 1024 * 1024
INPROJ_ROW_TILE = 1024
INPROJ_SUB = (1024, 256)

F32 = jnp.float32
BF16 = jnp.bfloat16


def _dot(a, b):
    return jnp.dot(a.astype(BF16), b.astype(BF16), preferred_element_type=F32)


def _dot_nt(a, b):
    return lax.dot_general(a.astype(BF16), b.astype(BF16), (((1,), (1,)), ((), ())),
                           preferred_element_type=F32)


def _dot_tn(a, b):
    return lax.dot_general(a.astype(BF16), b.astype(BF16), (((0,), (0,)), ((), ())),
                           preferred_element_type=F32)


def _dot_exact_lhs(a_twice, x):
    x1 = x.astype(BF16)
    x2 = (x - x1.astype(F32)).astype(BF16)
    return jnp.dot(a_twice, jnp.concatenate([x1, x2], axis=0), preferred_element_type=F32)


def _sigmoid(x):
    return 1.0 / (1.0 + jnp.exp(-x))


def _silu(x):
    return x * _sigmoid(x)


def _pad_rows(x):
    return jnp.concatenate([x, jnp.zeros_like(x)], axis=0)


def _head_norm_gate(o, nw, gate):
    return o * lax.rsqrt(jnp.mean(o * o, axis=-1, keepdims=True) + NORM_EPS) * nw * gate


_Q, _K, _V, _Z, _HQ, _HF, _HI, _HG = range(8)


def _inproj_kernel(x_ref, nw_ref, wg_ref, wh_ref, wab_ref, cw_ref, lb_ref, o_ref, logf_ref, oab_ref,
                   n_scr, halo_scr, *, tm, seq, layer):
    i = pl.program_id(0)
    j = pl.program_id(1)
    tn = o_ref.shape[1]

    @pl.when(j == 0)
    def _():
        x = x_ref[...]
        n = x * lax.rsqrt(jnp.mean(x * x, axis=-1, keepdims=True) + NORM_EPS) * nw_ref[...]
        nb = n.astype(BF16)
        n_scr[...] = nb
        oab_ref[...] = jnp.dot(nb, wab_ref[...], preferred_element_type=F32)

    sub_r, sub_c = min(INPROJ_SUB[0], tm), min(INPROJ_SUB[1], tn)
    blocks = [(slice(r, r + sub_r), slice(c, c + sub_c)) for r in range(0, tm, sub_r) for c in range(0, tn, sub_c)]

    def project(epilogue, w_ref):
        pending = None
        for rows, cols in blocks:
            acc = jnp.dot(n_scr[rows, :], w_ref[:, cols], preferred_element_type=F32)
            if pending is not None:
                epilogue(*pending)
            pending = (rows, cols, acc)
        epilogue(*pending)

    def conv_role(normalise):
        seq_start = (i * tm) % seq == 0
        halo_all = jnp.where(seq_start, 0.0, halo_scr[j])
        scale = jnp.where(j == _Q, HEAD_DIM ** -0.5, 1.0)
        halo_row = lax.broadcasted_iota(jnp.int32, (SUBLANES, sub_c), 0)
        tails = {}

        def epilogue(rows, cols, acc):
            w = cw_ref[:, cols]
            halo = halo_all[:, cols] if rows.start == 0 else tails[cols.start]
            tails[cols.start] = acc[sub_r - SUBLANES:sub_r]
            if rows.stop == tm:
                halo_scr[j, :, cols] = acc[sub_r - SUBLANES:sub_r]
            y = acc * w[CONV_WIDTH - 1:CONV_WIDTH, :]
            for s in range(1, CONV_WIDTH):
                rolled = pltpu.roll(acc, s, 0)
                head = jnp.where(halo_row < s, pltpu.roll(halo, s, 0), rolled[0:SUBLANES])
                shifted = jnp.concatenate([head, rolled[SUBLANES:]], axis=0)
                y = y + shifted * w[CONV_WIDTH - 1 - s:CONV_WIDTH - s, :]
            y = _silu(y)
            if not normalise:
                o_ref[rows, cols] = y.astype(BF16)
                return
            for h in range(sub_c // HEAD_DIM):
                yh = y[:, h * HEAD_DIM:(h + 1) * HEAD_DIM]
                inv_norm = lax.rsqrt(jnp.sum(yh * yh, axis=-1, keepdims=True) + L2_EPS)
                hs = slice(cols.start + h * HEAD_DIM, cols.start + (h + 1) * HEAD_DIM)
                o_ref[rows, hs] = (yh * (inv_norm * scale)).astype(BF16)

        project(epilogue, wg_ref)

    @pl.when(j < _V)
    def _():
        conv_role(True)

    @pl.when(j == _V)
    def _():
        conv_role(False)

    def silu_epilogue(rows, cols, acc):
        o_ref[rows, cols] = _silu(acc).astype(BF16)

    @pl.when(j == _Z)
    def _():
        project(silu_epilogue, wg_ref)

    @pl.when((j == _HQ) | (j == _HG))
    def _():
        project(silu_epilogue, wh_ref)

    @pl.when(j == _HI)
    def _():
        def epilogue(rows, cols, acc):
            o_ref[rows, cols] = acc.astype(BF16)
        project(epilogue, wh_ref)

    @pl.when(j == _HF)
    def _():
        logits = lb_ref[...]
        e = jnp.exp(logits - jnp.max(logits, axis=0, keepdims=True))
        lb_all = jnp.sum(e[:layer + 1], axis=0, keepdims=True) / jnp.sum(e, axis=0, keepdims=True)

        def epilogue(rows, cols, acc):
            lb = lb_all[:, cols]
            sig = _sigmoid(acc)
            o_ref[rows, cols] = ((1.0 - lb) * (1.0 - sig)).astype(BF16)
            logf_ref[rows, cols] = jnp.log(lb + (1.0 - lb) * sig)
        project(epilogue, wh_ref)


def _inproj(x2, norm_w, w_gdn, w_hgrn, w_ab, conv_w, lb_logits, seq, layer, tm, tn):
    m, d = x2.shape
    n = w_gdn.shape[1] + w_hgrn.shape[1]
    assert w_gdn.shape[1] == (_Z + 1) * tn and n == 8 * tn and seq % tm == 0
    return pl.pallas_call(
        functools.partial(_inproj_kernel, tm=tm, seq=seq, layer=layer),
        grid=(m // tm, n // tn),
        in_specs=[
            pl.BlockSpec((tm, d), lambda i, j: (i, 0)),
            pl.BlockSpec((1, d), lambda i, j: (0, 0)),
            pl.BlockSpec((d, tn), lambda i, j: (0, jnp.minimum(j, _Z))),
            pl.BlockSpec((d, tn), lambda i, j: (0, jnp.maximum(j, _HQ) - _HQ)),
            pl.BlockSpec((d, HEAD_DIM), lambda i, j: (0, 0)),
            pl.BlockSpec((CONV_WIDTH, tn), lambda i, j: (0, jnp.minimum(j, _V))),
            pl.BlockSpec((lb_logits.shape[0], tn), lambda i, j: (0, 0)),
        ],
        out_specs=[
            pl.BlockSpec((tm, tn), lambda i, j: (i, j)),
            pl.BlockSpec((tm, tn), lambda i, j: (i, 0)),
            pl.BlockSpec((tm, HEAD_DIM), lambda i, j: (i, 0)),
        ],
        out_shape=[jax.ShapeDtypeStruct((m, n), BF16), jax.ShapeDtypeStruct((m, tn), F32),
                   jax.ShapeDtypeStruct((m, HEAD_DIM), F32)],
        scratch_shapes=[pltpu.VMEM((tm, d), BF16), pltpu.VMEM((_V + 1, SUBLANES, tn), F32)],
        compiler_params=pltpu.CompilerParams(
            dimension_semantics=("arbitrary", "arbitrary"), vmem_limit_bytes=BIG_VMEM_LIMIT_BYTES),
    )(x2, norm_w.reshape(1, d), w_gdn, w_hgrn, w_ab, conv_w, lb_logits)


def _gdn_kernel(q_ref, k_ref, v_ref, z_ref, ab_ref, gp_ref, nw_ref, w1_ref, w2_ref, y_ref, w1b_ref, w2b_ref,
                s_scr, *, n_heads, hb, tb):
    h0 = pl.program_id(1) * hb
    w1b_ref[...] = w1_ref[...].astype(BF16)
    w2b_ref[...] = w2_ref[...].astype(BF16)

    @pl.when(pl.program_id(2) == 0)
    def _():
        s_scr[...] = jnp.zeros_like(s_scr)

    ab = ab_ref[...]
    gp = gp_ref[...]
    sp_in = ab + gp[1:2, :]
    g_tile = -jnp.exp(gp[0:1, :]) * (jnp.maximum(sp_in, 0.0) + jnp.log1p(jnp.exp(-jnp.abs(sp_in))))
    beta_tile = _sigmoid(ab)
    lane_t = lax.broadcasted_iota(jnp.int32, ab.shape, 1)

    def head_column(tile, idx):
        return jnp.broadcast_to(jnp.sum(jnp.where(lane_t == idx, tile, 0.0), axis=-1, keepdims=True), tile.shape)

    g_heads = [head_column(g_tile, h0 + j) for j in range(hb)]
    beta_heads = [head_column(beta_tile, h0 + j + n_heads) for j in range(hb)]

    row = lax.broadcasted_iota(jnp.int32, (CHUNK, HEAD_DIM), 0)
    lane = lax.broadcasted_iota(jnp.int32, (CHUNK, HEAD_DIM), 1)
    col = lane & (CHUNK - 1)
    left = lane < CHUNK
    incl = row >= col
    strict = row > col
    blk16 = (row >> 4) == (col >> 4)
    blk32 = (row >> 5) == (col >> 5)
    tri_twice = (row >= (lane & (CHUNK - 1))).astype(BF16)
    eye_left = jnp.where(lane == row, 1.0, 0.0)
    zeros = jnp.zeros((CHUNK, HEAD_DIM), F32)
    nw = nw_ref[...]

    n_chunks = tb // CHUNK
    items = [(j, c) for c in range(n_chunks) for j in range(hb)]
    rs = [slice(c * CHUNK, (c + 1) * CHUNK) for _, c in items]
    hs = [slice(j * HEAD_DIM, (j + 1) * HEAD_DIM) for j, _ in items]
    betas = [beta_heads[j][r] for (j, _), r in zip(items, rs)]
    gs = [g_heads[j][r] for (j, _), r in zip(items, rs)]
    dgs = [_dot_exact_lhs(tri_twice, jnp.concatenate([jnp.where(strict, g, 0.0), g], axis=1)) for g in gs]
    gcums = [dg[:, HEAD_DIM:] for dg in dgs]
    decays = [jnp.where(incl, jnp.exp(dg[:, :HEAD_DIM]), 0.0) for dg in dgs]
    g_lasts = [g[CHUNK - 1:CHUNK, :] for g in gcums]
    e_gs = [jnp.exp(g) for g in gcums]
    sds = [jnp.exp(gl) for gl in g_lasts]

    qbs = [q_ref[r, h] for r, h in zip(rs, hs)]
    kbs = [k_ref[r, h] for r, h in zip(rs, hs)]
    qns = [q.astype(F32) for q in qbs]
    kns = [k.astype(F32) for k in kbs]
    qks = [_dot_nt(jnp.concatenate([q, k], axis=0), jnp.concatenate([k, k], axis=0)) for q, k in zip(qbs, kbs)]
    attns = [jnp.where(left, qk[:CHUNK] * d, 0.0) for qk, d in zip(qks, decays)]
    lms = [jnp.where(strict & (~left), b * qk[CHUNK:] * d, 0.0) for b, qk, d in zip(betas, qks, decays)]

    ts = [eye_left - jnp.where(blk16, lm, 0.0) for lm in lms]
    for _ in range(4):
        ts = [_dot(t, jnp.concatenate([zeros, t], axis=0)) + jnp.where(left, t, 0.0) for t in ts]
    invs = [jnp.where(left, t, 0.0) for t in ts]
    for sel in (blk32 & (~blk16), ~blk32):
        w1s = [_dot(jnp.where(sel, lm, 0.0), jnp.concatenate([zeros, inv], axis=0)) for lm, inv in zip(lms, invs)]
        invs = [inv - _dot(inv, _pad_rows(w1)) for inv, w1 in zip(invs, w1s)]

    sols = [_dot(inv, _pad_rows(jnp.concatenate([b * e_g * kn, b * v_ref[r, h].astype(F32)], axis=1)))
            for inv, b, e_g, kn, r, h in zip(invs, betas, e_gs, kns, rs, hs)]
    k_ends = [kn * jnp.exp(gl - g) for kn, gl, g in zip(kns, g_lasts, gcums)]
    kws = [_dot_tn(k_end, sol) for k_end, sol in zip(k_ends, sols)]
    aws = [_dot(attn, _pad_rows(sol)) for attn, sol in zip(attns, sols)]
    lhss = [jnp.concatenate([kw[:, :HEAD_DIM], qn * e_g - aw[:, :HEAD_DIM]], axis=0)
            for kw, qn, e_g, aw in zip(kws, qns, e_gs, aws)]

    states = [s_scr[j] for j in range(hb)]
    for i, (j, _) in enumerate(items):
        r = _dot(lhss[i], states[j])
        o = r[HEAD_DIM:] + aws[i][:, HEAD_DIM:]
        states[j] = states[j] * sds[i] - r[:HEAD_DIM] + kws[i][:, HEAD_DIM:]
        y_ref[rs[i], hs[i]] = _head_norm_gate(o, nw, z_ref[rs[i], hs[i]].astype(F32)).astype(y_ref.dtype)
    for j in range(hb):
        s_scr[j] = states[j]


def _gdn(act, ab, gate_par, norm_w, w1, w2, batch, seq, n_heads, hb, tb):
    m = act.shape[0]
    nt = seq // tb
    ng = n_heads // hb
    width = hb * HEAD_DIM
    steps = batch * ng * nt
    r1, r2 = w1.shape[0] // steps, w2.shape[0] // steps
    assert r1 * steps == w1.shape[0] and r2 * steps == w2.shape[0] and r1 % 16 == 0 and r2 % 16 == 0
    tok = lambda part: pl.BlockSpec((tb, width), lambda b, g, t: (b * nt + t, part * ng + g))
    slab = lambda rows, cols: pl.BlockSpec((rows, cols), lambda b, g, t: ((b * ng + g) * nt + t, 0))
    return pl.pallas_call(
        functools.partial(_gdn_kernel, n_heads=n_heads, hb=hb, tb=tb),
        grid=(batch, ng, nt),
        in_specs=[
            tok(_Q), tok(_K), tok(_V), tok(_Z),
            pl.BlockSpec((tb, HEAD_DIM), lambda b, g, t: (b * nt + t, 0)),
            pl.BlockSpec((SUBLANES, HEAD_DIM), lambda b, g, t: (0, 0)),
            pl.BlockSpec((1, HEAD_DIM), lambda b, g, t: (0, 0)),
            slab(r1, w1.shape[1]), slab(r2, w2.shape[1]),
        ],
        out_specs=[pl.BlockSpec((tb, width), lambda b, g, t: (b * nt + t, g)),
                   slab(r1, w1.shape[1]), slab(r2, w2.shape[1])],
        out_shape=[jax.ShapeDtypeStruct((m, n_heads * HEAD_DIM), BF16),
                   jax.ShapeDtypeStruct(w1.shape, BF16), jax.ShapeDtypeStruct(w2.shape, BF16)],
        scratch_shapes=[pltpu.VMEM((hb, HEAD_DIM, HEAD_DIM), F32)],
        compiler_params=pltpu.CompilerParams(
            dimension_semantics=("arbitrary", "arbitrary", "arbitrary"), vmem_limit_bytes=VMEM_LIMIT_BYTES),
    )(act, act, act, act, ab, gate_par, norm_w.reshape(1, HEAD_DIM), w1, w2)


def _hgrn_kernel(q_ref, k_ref, i_ref, g_ref, logf_ref, nw_ref, y_ref, st_scr, *, hb, tb):
    @pl.when(pl.program_id(2) == 0)
    def _():
        st_scr[...] = jnp.zeros_like(st_scr)

    row = lax.broadcasted_iota(jnp.int32, (CHUNK, HEAD_DIM), 0)
    lane = lax.broadcasted_iota(jnp.int32, (CHUNK, HEAD_DIM), 1)
    tri_twice = (row >= (lane & (CHUNK - 1))).astype(BF16)
    nw = nw_ref[...]

    n_chunks = tb // CHUNK
    items = [(j, c) for c in range(n_chunks) for j in range(hb)]
    ids = range(len(items))
    rs = [slice(c * CHUNK, (c + 1) * CHUNK) for _, c in items]
    hs = [slice(j * HEAD_DIM, (j + 1) * HEAD_DIM) for j, _ in items]
    keys = [k_ref[r, h].astype(F32) for r, h in zip(rs, hs)]
    qss = [q_ref[r, h].astype(F32) for r, h in zip(rs, hs)]
    vbs = [i_ref[r, h] for r, h in zip(rs, hs)]
    vs = [v.astype(F32) for v in vbs]
    bs = [_dot_exact_lhs(tri_twice, logf_ref[r, h]) for r, h in zip(rs, hs)]
    b_lasts = [b[CHUNK - 1:CHUNK, :] for b in bs]

    a_s = [jnp.zeros((CHUNK, HEAD_DIM), F32) for _ in ids]
    n = CHUNK // 2
    while n >= 1:
        upper = (row & n) != 0
        keep = ((row & -(2 * n)) == (lane & -(2 * n))) & upper & ((lane & n) == 0)
        offset = row & (2 * n - 1)
        for i in ids:
            b = bs[i]
            if 2 * n >= 8:
                b_ref = jnp.concatenate(
                    [jnp.broadcast_to(b[r0 + n - 1:r0 + n, :], (2 * n, HEAD_DIM)) for r0 in range(0, CHUNK, 2 * n)],
                    axis=0)
            else:
                b_ref = b
                for o in range(2 * n):
                    if o != n - 1:
                        b_ref = jnp.where(offset == o, pltpu.roll(b, (o - (n - 1)) % CHUNK, 0), b_ref)
            x = jnp.where(upper, qss[i], keys[i]) * jnp.exp(-jnp.abs(b - b_ref))
            a_s[i] = a_s[i] + jnp.where(keep, _dot_nt(x, _pad_rows(x)), 0.0)
        n //= 2

    intras = [_dot(a, _pad_rows(vb)) + jnp.sum(qs * key, axis=-1, keepdims=True) * v
              for a, vb, v, qs, key in zip(a_s, vbs, vs, qss, keys)]
    kvs = [_dot_tn(vb, key * jnp.exp(bl - b)) for vb, key, bl, b in zip(vbs, keys, b_lasts, bs)]
    qis = [qs * jnp.exp(b) for qs, b in zip(qss, bs)]

    states = [st_scr[j] for j in range(hb)]
    for i, (j, _) in enumerate(items):
        o = _dot_nt(qis[i], states[j]) + intras[i]
        states[j] = states[j] * jnp.exp(b_lasts[i]) + kvs[i]
        y_ref[rs[i], hs[i]] = _head_norm_gate(o, nw, g_ref[rs[i], hs[i]].astype(F32)).astype(y_ref.dtype)
    for j in range(hb):
        st_scr[j] = states[j]


def _hgrn(act, logf, norm_w, batch, seq, n_heads, hb, tb):
    m = act.shape[0]
    nt = seq // tb
    ng = n_heads // hb
    width = hb * HEAD_DIM
    tok = lambda part: pl.BlockSpec((tb, width), lambda b, g, t: (b * nt + t, part * ng + g))
    return pl.pallas_call(
        functools.partial(_hgrn_kernel, hb=hb, tb=tb),
        grid=(batch, ng, nt),
        in_specs=[
            tok(_HQ), tok(_HF), tok(_HI), tok(_HG),
            pl.BlockSpec((tb, width), lambda b, g, t: (b * nt + t, g)),
            pl.BlockSpec((1, HEAD_DIM), lambda b, g, t: (0, 0)),
        ],
        out_specs=pl.BlockSpec((tb, width), lambda b, g, t: (b * nt + t, g)),
        out_shape=jax.ShapeDtypeStruct((m, n_heads * HEAD_DIM), BF16),
        scratch_shapes=[pltpu.VMEM((hb, HEAD_DIM, HEAD_DIM), F32)],
        compiler_params=pltpu.CompilerParams(
            dimension_semantics=("arbitrary", "arbitrary", "arbitrary"), vmem_limit_bytes=VMEM_LIMIT_BYTES),
    )(act, act, act, act, logf, norm_w.reshape(1, HEAD_DIM))


def _outproj_kernel(ya_ref, yb_ref, wa_ref, wb_ref, x_ref, o_ref):
    acc = jnp.dot(ya_ref[...], wa_ref[...], preferred_element_type=F32)
    acc = acc + jnp.dot(yb_ref[...], wb_ref[...], preferred_element_type=F32)
    o_ref[...] = x_ref[...] + acc


def _outproj(ya, yb, w_a, w_b, x2, tm):
    m, d = x2.shape
    return pl.pallas_call(
        _outproj_kernel,
        grid=(m // tm,),
        in_specs=[
            pl.BlockSpec((tm, ya.shape[1]), lambda i: (i, 0)),
            pl.BlockSpec((tm, yb.shape[1]), lambda i: (i, 0)),
            pl.BlockSpec(w_a.shape, lambda i: (0, 0)),
            pl.BlockSpec(w_b.shape, lambda i: (0, 0)),
            pl.BlockSpec((tm, d), lambda i: (i, 0)),
        ],
        out_specs=pl.BlockSpec((tm, d), lambda i: (i, 0)),
        out_shape=jax.ShapeDtypeStruct((m, d), F32),
        compiler_params=pltpu.CompilerParams(
            dimension_semantics=("arbitrary",), vmem_limit_bytes=VMEM_LIMIT_BYTES),
    )(ya, yb, w_a, w_b, x2)


def _mlp_kernel(h_ref, nw_ref, w1_ref, w2_ref, fw_ref, o_ref, n_scr, *, final_norm):
    f = pl.program_id(1)

    @pl.when(f == 0)
    def _():
        x = h_ref[...]
        n = x * lax.rsqrt(jnp.mean(x * x, axis=-1, keepdims=True) + NORM_EPS) * nw_ref[...]
        n_scr[...] = n.astype(BF16)
        o_ref[...] = x

    hid = jnp.dot(n_scr[...], w1_ref[...], preferred_element_type=F32)
    hid = jnp.square(jnp.maximum(hid, 0.0)).astype(BF16)
    o_ref[...] += jnp.dot(hid, w2_ref[...], preferred_element_type=F32)

    if final_norm:
        @pl.when(f == pl.num_programs(1) - 1)
        def _():
            y = o_ref[...]
            o_ref[...] = y * lax.rsqrt(jnp.mean(y * y, axis=-1, keepdims=True) + NORM_EPS) * fw_ref[...]


def _mlp(h, norm_w, w1, w2, final_w, final_norm, tm, tf):
    m, d = h.shape
    ff = w1.shape[1]
    return pl.pallas_call(
        functools.partial(_mlp_kernel, final_norm=final_norm),
        grid=(m // tm, ff // tf),
        in_specs=[
            pl.BlockSpec((tm, d), lambda i, f: (i, 0)),
            pl.BlockSpec((1, d), lambda i, f: (0, 0)),
            pl.BlockSpec((d, tf), lambda i, f: (0, f)),
            pl.BlockSpec((tf, d), lambda i, f: (f, 0)),
            pl.BlockSpec((1, d), lambda i, f: (0, 0)),
        ],
        out_specs=pl.BlockSpec((tm, d), lambda i, f: (i, 0)),
        out_shape=jax.ShapeDtypeStruct((m, d), F32),
        scratch_shapes=[pltpu.VMEM((tm, d), BF16)],
        compiler_params=pltpu.CompilerParams(
            dimension_semantics=("arbitrary", "arbitrary"), vmem_limit_bytes=BIG_VMEM_LIMIT_BYTES),
    )(h, norm_w.reshape(1, d), w1, w2, final_w.reshape(1, d))


def _tile(n, target):
    t = min(n, target)
    while n % t:
        t //= 2
    return t


def kernel(x, w_in, conv_w, gdn_a_log, gdn_dt_bias, gdn_norm_w, hgrn_lb_logits, hgrn_norm_w, w_out,
           norm_mix_w, norm_ffn_w, w_ff1, w_ff2, norm_final_w):
    batch, seq, d_model = x.shape
    depth = w_in.shape[0]
    gh = gdn_a_log.shape[1]
    hh = hgrn_lb_logits.shape[1] // HEAD_DIM
    gw, hw = gh * HEAD_DIM, hh * HEAD_DIM
    assert 2 * gh <= HEAD_DIM and seq % CHUNK == 0
    assert gw == hw and w_in.shape[2] == 4 * gw + 2 * gh + 4 * hw

    m = batch * seq
    tm = _tile(m, ROW_TILE)
    gdn_hb, gdn_tb = _tile(gh, GDN_BLOCK[0]), _tile(seq, GDN_BLOCK[1])
    hgrn_hb, hgrn_tb = _tile(hh, HGRN_BLOCK[0]), _tile(seq, HGRN_BLOCK[1])
    h = x.reshape(m, d_model)
    for layer in range(depth):
        wl = w_in[layer]
        o1 = 4 * gw
        o2 = o1 + 2 * gh
        w_gdn = wl[:, :o1].astype(BF16)
        wl, w_gdn = lax.optimization_barrier((wl, w_gdn))
        w_hgrn = wl[:, o2:].astype(BF16)
        w_ab = jnp.pad(wl[:, o1:o2], ((0, 0), (0, HEAD_DIM - 2 * gh))).astype(BF16)
        gate_par = jnp.zeros((SUBLANES, HEAD_DIM), F32)
        gate_par = gate_par.at[0, :gh].set(gdn_a_log[layer]).at[1, :gh].set(gdn_dt_bias[layer])
        w_o = w_out[layer].astype(BF16)

        act, logf, ab = _inproj(h, norm_mix_w[layer], w_gdn, w_hgrn, w_ab, conv_w[layer], hgrn_lb_logits, seq,
                                layer, _tile(seq, INPROJ_ROW_TILE), gw)
        y_a, w1, w2 = _gdn(act, ab, gate_par, gdn_norm_w[layer], w_ff1[layer], w_ff2[layer], batch, seq, gh,
                           gdn_hb, gdn_tb)
        y_b = _hgrn(act, logf, hgrn_norm_w[layer], batch, seq, hh, hgrn_hb, hgrn_tb)
        h = _outproj(y_a, y_b, w_o[:gw], w_o[gw:], h, tm)
        h = _mlp(h, norm_ffn_w[layer], w1, w2, norm_final_w, layer == depth - 1, _tile(m, MLP_TILE[0]),
                 _tile(w1.shape[1], MLP_TILE[1]))
    return h.reshape(batch, seq, d_model)
```

```python
import functools

import jax
import jax.numpy as jnp
from jax import lax
from jax.experimental import pallas as pl
from jax.experimental.pallas import tpu as pltpu

HEAD_DIM = 128
SUBLANES = 8
CHUNK = 64
CONV_WIDTH = 4
NORM_EPS = 1e-6
L2_EPS = 1e-6
VMEM_LIMIT_BYTES = 56 * 1024 * 1024

GDN_BLOCK = (8, 512)
HGRN_BLOCK = (8, 256)
ROW_TILE = 512
MLP_TILE = (1024, 1024)
BIG_VMEM_LIMIT_BYTES = 62 * 1024 * 1024
INPROJ_ROW_TILE = 1024
INPROJ_SUB = (1024, 256)

F32 = jnp.float32
BF16 = jnp.bfloat16


def _dot(a, b):
    return jnp.dot(a.astype(BF16), b.astype(BF16), preferred_element_type=F32)


def _dot_nt(a, b):
    return lax.dot_general(a.astype(BF16), b.astype(BF16), (((1,), (1,)), ((), ())),
                           preferred_element_type=F32)


def _dot_tn(a, b):
    return lax.dot_general(a.astype(BF16), b.astype(BF16), (((0,), (0,)), ((), ())),
                           preferred_element_type=F32)


def _dot_exact_lhs(a_twice, x):
    x1 = x.astype(BF16)
    x2 = (x - x1.astype(F32)).astype(BF16)
    return jnp.dot(a_twice, jnp.concatenate([x1, x2], axis=0), preferred_element_type=F32)


def _sigmoid(x):
    return 1.0 / (1.0 + jnp.exp(-x))


def _silu(x):
    return x * _sigmoid(x)


def _pad_rows(x):
    return jnp.concatenate([x, jnp.zeros_like(x)], axis=0)


def _head_norm_gate(o, nw, gate):
    return o * lax.rsqrt(jnp.mean(o * o, axis=-1, keepdims=True) + NORM_EPS) * nw * gate


_Q, _K, _V, _Z, _HQ, _HF, _HI, _HG = range(8)


def _inproj_kernel(x_ref, nw_ref, wg_ref, wh_ref, wab_ref, cw_ref, lb_ref, o_ref, logf_ref, oab_ref,
                   n_scr, halo_scr, *, tm, seq, layer):
    i = pl.program_id(0)
    j = pl.program_id(1)
    tn = o_ref.shape[1]

    @pl.when(j == 0)
    def _():
        x = x_ref[...]
        n = x * lax.rsqrt(jnp.mean(x * x, axis=-1, keepdims=True) + NORM_EPS) * nw_ref[...]
        nb = n.astype(BF16)
        n_scr[...] = nb
        oab_ref[...] = jnp.dot(nb, wab_ref[...], preferred_element_type=F32)

    sub_r, sub_c = min(INPROJ_SUB[0], tm), min(INPROJ_SUB[1], tn)
    blocks = [(slice(r, r + sub_r), slice(c, c + sub_c)) for r in range(0, tm, sub_r) for c in range(0, tn, sub_c)]

    def project(epilogue, w_ref):
        pending = None
        for rows, cols in blocks:
            acc = jnp.dot(n_scr[rows, :], w_ref[:, cols], preferred_element_type=F32)
            if pending is not None:
                epilogue(*pending)
            pending = (rows, cols, acc)
        epilogue(*pending)

    def conv_role(normalise):
        seq_start = (i * tm) % seq == 0
        halo_all = jnp.where(seq_start, 0.0, halo_scr[j])
        scale = jnp.where(j == _Q, HEAD_DIM ** -0.5, 1.0)
        halo_row = lax.broadcasted_iota(jnp.int32, (SUBLANES, sub_c), 0)
        tails = {}

        def epilogue(rows, cols, acc):
            w = cw_ref[:, cols]
            halo = halo_all[:, cols] if rows.start == 0 else tails[cols.start]
            tails[cols.start] = acc[sub_r - SUBLANES:sub_r]
            if rows.stop == tm:
                halo_scr[j, :, cols] = acc[sub_r - SUBLANES:sub_r]
            y = acc * w[CONV_WIDTH - 1:CONV_WIDTH, :]
            for s in range(1, CONV_WIDTH):
                rolled = pltpu.roll(acc, s, 0)
                head = jnp.where(halo_row < s, pltpu.roll(halo, s, 0), rolled[0:SUBLANES])
                shifted = jnp.concatenate([head, rolled[SUBLANES:]], axis=0)
                y = y + shifted * w[CONV_WIDTH - 1 - s:CONV_WIDTH - s, :]
            y = _silu(y)
            if not normalise:
                o_ref[rows, cols] = y.astype(BF16)
                return
            for h in range(sub_c // HEAD_DIM):
                yh = y[:, h * HEAD_DIM:(h + 1) * HEAD_DIM]
                inv_norm = lax.rsqrt(jnp.sum(yh * yh, axis=-1, keepdims=True) + L2_EPS)
                hs = slice(cols.start + h * HEAD_DIM, cols.start + (h + 1) * HEAD_DIM)
                o_ref[rows, hs] = (yh * (inv_norm * scale)).astype(BF16)

        project(epilogue, wg_ref)

    @pl.when(j < _V)
    def _():
        conv_role(True)

    @pl.when(j == _V)
    def _():
        conv_role(False)

    def silu_epilogue(rows, cols, acc):
        o_ref[rows, cols] = _silu(acc).astype(BF16)

    @pl.when(j == _Z)
    def _():
        project(silu_epilogue, wg_ref)

    @pl.when((j == _HQ) | (j == _HG))
    def _():
        project(silu_epilogue, wh_ref)

    @pl.when(j == _HI)
    def _():
        def epilogue(rows, cols, acc):
            o_ref[rows, cols] = acc.astype(BF16)
        project(epilogue, wh_ref)

    @pl.when(j == _HF)
    def _():
        logits = lb_ref[...]
        e = jnp.exp(logits - jnp.max(logits, axis=0, keepdims=True))
        lb_all = jnp.sum(e[:layer + 1], axis=0, keepdims=True) / jnp.sum(e, axis=0, keepdims=True)

        def epilogue(rows, cols, acc):
            lb = lb_all[:, cols]
            sig = _sigmoid(acc)
            o_ref[rows, cols] = ((1.0 - lb) * (1.0 - sig)).astype(BF16)
            logf_ref[rows, cols] = jnp.log(lb + (1.0 - lb) * sig)
        project(epilogue, wh_ref)


def _inproj(x2, norm_w, w_gdn, w_hgrn, w_ab, conv_w, lb_logits, seq, layer, tm, tn):
    m, d = x2.shape
    n = w_gdn.shape[1] + w_hgrn.shape[1]
    assert w_gdn.shape[1] == (_Z + 1) * tn and n == 8 * tn and seq % tm == 0
    return pl.pallas_call(
        functools.partial(_inproj_kernel, tm=tm, seq=seq, layer=layer),
        grid=(m // tm, n // tn),
        in_specs=[
            pl.BlockSpec((tm, d), lambda i, j: (i, 0)),
            pl.BlockSpec((1, d), lambda i, j: (0, 0)),
            pl.BlockSpec((d, tn), lambda i, j: (0, jnp.minimum(j, _Z))),
            pl.BlockSpec((d, tn), lambda i, j: (0, jnp.maximum(j, _HQ) - _HQ)),
            pl.BlockSpec((d, HEAD_DIM), lambda i, j: (0, 0)),
            pl.BlockSpec((CONV_WIDTH, tn), lambda i, j: (0, jnp.minimum(j, _V))),
            pl.BlockSpec((lb_logits.shape[0], tn), lambda i, j: (0, 0)),
        ],
        out_specs=[
            pl.BlockSpec((tm, tn), lambda i, j: (i, j)),
            pl.BlockSpec((tm, tn), lambda i, j: (i, 0)),
            pl.BlockSpec((tm, HEAD_DIM), lambda i, j: (i, 0)),
        ],
        out_shape=[jax.ShapeDtypeStruct((m, n), BF16), jax.ShapeDtypeStruct((m, tn), F32),
                   jax.ShapeDtypeStruct((m, HEAD_DIM), F32)],
        scratch_shapes=[pltpu.VMEM((tm, d), BF16), pltpu.VMEM((_V + 1, SUBLANES, tn), F32)],
        compiler_params=pltpu.CompilerParams(
            dimension_semantics=("arbitrary", "arbitrary"), vmem_limit_bytes=BIG_VMEM_LIMIT_BYTES),
    )(x2, norm_w.reshape(1, d), w_gdn, w_hgrn, w_ab, conv_w, lb_logits)


def _gdn_kernel(q_ref, k_ref, v_ref, z_ref, ab_ref, gp_ref, nw_ref, w1_ref, w2_ref, y_ref, w1b_ref, w2b_ref,
                s_scr, *, n_heads, hb, tb):
    h0 = pl.program_id(1) * hb
    w1b_ref[...] = w1_ref[...].astype(BF16)
    w2b_ref[...] = w2_ref[...].astype(BF16)

    @pl.when(pl.program_id(2) == 0)
    def _():
        s_scr[...] = jnp.zeros_like(s_scr)

    ab = ab_ref[...]
    gp = gp_ref[...]
    sp_in = ab + gp[1:2, :]
    g_tile = -jnp.exp(gp[0:1, :]) * (jnp.maximum(sp_in, 0.0) + jnp.log1p(jnp.exp(-jnp.abs(sp_in))))
    beta_tile = _sigmoid(ab)
    lane_t = lax.broadcasted_iota(jnp.int32, ab.shape, 1)

    def head_column(tile, idx):
        return jnp.broadcast_to(jnp.sum(jnp.where(lane_t == idx, tile, 0.0), axis=-1, keepdims=True), tile.shape)

    g_heads = [head_column(g_tile, h0 + j) for j in range(hb)]
    beta_heads = [head_column(beta_tile, h0 + j + n_heads) for j in range(hb)]

    row = lax.broadcasted_iota(jnp.int32, (CHUNK, HEAD_DIM), 0)
    lane = lax.broadcasted_iota(jnp.int32, (CHUNK, HEAD_DIM), 1)
    col = lane & (CHUNK - 1)
    left = lane < CHUNK
    incl = row >= col
    strict = row > col
    blk16 = (row >> 4) == (col >> 4)
    blk32 = (row >> 5) == (col >> 5)
    tri_twice = (row >= (lane & (CHUNK - 1))).astype(BF16)
    eye_left = jnp.where(lane == row, 1.0, 0.0)
    zeros = jnp.zeros((CHUNK, HEAD_DIM), F32)
    nw = nw_ref[...]

    n_chunks = tb // CHUNK
    items = [(j, c) for c in range(n_chunks) for j in range(hb)]
    rs = [slice(c * CHUNK, (c + 1) * CHUNK) for _, c in items]
    hs = [slice(j * HEAD_DIM, (j + 1) * HEAD_DIM) for j, _ in items]
    betas = [beta_heads[j][r] for (j, _), r in zip(items, rs)]
    gs = [g_heads[j][r] for (j, _), r in zip(items, rs)]
    dgs = [_dot_exact_lhs(tri_twice, jnp.concatenate([jnp.where(strict, g, 0.0), g], axis=1)) for g in gs]
    gcums = [dg[:, HEAD_DIM:] for dg in dgs]
    decays = [jnp.where(incl, jnp.exp(dg[:, :HEAD_DIM]), 0.0) for dg in dgs]
    g_lasts = [g[CHUNK - 1:CHUNK, :] for g in gcums]
    e_gs = [jnp.exp(g) for g in gcums]
    sds = [jnp.exp(gl) for gl in g_lasts]

    qbs = [q_ref[r, h] for r, h in zip(rs, hs)]
    kbs = [k_ref[r, h] for r, h in zip(rs, hs)]
    qns = [q.astype(F32) for q in qbs]
    kns = [k.astype(F32) for k in kbs]
    qks = [_dot_nt(jnp.concatenate([q, k], axis=0), jnp.concatenate([k, k], axis=0)) for q, k in zip(qbs, kbs)]
    attns = [jnp.where(left, qk[:CHUNK] * d, 0.0) for qk, d in zip(qks, decays)]
    lms = [jnp.where(strict & (~left), b * qk[CHUNK:] * d, 0.0) for b, qk, d in zip(betas, qks, decays)]

    ts = [eye_left - jnp.where(blk16, lm, 0.0) for lm in lms]
    for _ in range(4):
        ts = [_dot(t, jnp.concatenate([zeros, t], axis=0)) + jnp.where(left, t, 0.0) for t in ts]
    invs = [jnp.where(left, t, 0.0) for t in ts]
    for sel in (blk32 & (~blk16), ~blk32):
        w1s = [_dot(jnp.where(sel, lm, 0.0), jnp.concatenate([zeros, inv], axis=0)) for lm, inv in zip(lms, invs)]
        invs = [inv - _dot(inv, _pad_rows(w1)) for inv, w1 in zip(invs, w1s)]

    sols = [_dot(inv, _pad_rows(jnp.concatenate([b * e_g * kn, b * v_ref[r, h].astype(F32)], axis=1)))
            for inv, b, e_g, kn, r, h in zip(invs, betas, e_gs, kns, rs, hs)]
    k_ends = [kn * jnp.exp(gl - g) for kn, gl, g in zip(kns, g_lasts, gcums)]
    kws = [_dot_tn(k_end, sol) for k_end, sol in zip(k_ends, sols)]
    aws = [_dot(attn, _pad_rows(sol)) for attn, sol in zip(attns, sols)]
    lhss = [jnp.concatenate([kw[:, :HEAD_DIM], qn * e_g - aw[:, :HEAD_DIM]], axis=0)
            for kw, qn, e_g, aw in zip(kws, qns, e_gs, aws)]

    states = [s_scr[j] for j in range(hb)]
    for i, (j, _) in enumerate(items):
        r = _dot(lhss[i], states[j])
        o = r[HEAD_DIM:] + aws[i][:, HEAD_DIM:]
        states[j] = states[j] * sds[i] - r[:HEAD_DIM] + kws[i][:, HEAD_DIM:]
        y_ref[rs[i], hs[i]] = _head_norm_gate(o, nw, z_ref[rs[i], hs[i]].astype(F32)).astype(y_ref.dtype)
    for j in range(hb):
        s_scr[j] = states[j]


def _gdn(act, ab, gate_par, norm_w, w1, w2, batch, seq, n_heads, hb, tb):
    m = act.shape[0]
    nt = seq // tb
    ng = n_heads // hb
    width = hb * HEAD_DIM
    steps = batch * ng * nt
    r1, r2 = w1.shape[0] // steps, w2.shape[0] // steps
    assert r1 * steps == w1.shape[0] and r2 * steps == w2.shape[0] and r1 % 16 == 0 and r2 % 16 == 0
    tok = lambda part: pl.BlockSpec((tb, width), lambda b, g, t: (b * nt + t, part * ng + g))
    slab = lambda rows, cols: pl.BlockSpec((rows, cols), lambda b, g, t: ((b * ng + g) * nt + t, 0))
    return pl.pallas_call(
        functools.partial(_gdn_kernel, n_heads=n_heads, hb=hb, tb=tb),
        grid=(batch, ng, nt),
        in_specs=[
            tok(_Q), tok(_K), tok(_V), tok(_Z),
            pl.BlockSpec((tb, HEAD_DIM), lambda b, g, t: (b * nt + t, 0)),
            pl.BlockSpec((SUBLANES, HEAD_DIM), lambda b, g, t: (0, 0)),
            pl.BlockSpec((1, HEAD_DIM), lambda b, g, t: (0, 0)),
            slab(r1, w1.shape[1]), slab(r2, w2.shape[1]),
        ],
        out_specs=[pl.BlockSpec((tb, width), lambda b, g, t: (b * nt + t, g)),
                   slab(r1, w1.shape[1]), slab(r2, w2.shape[1])],
        out_shape=[jax.ShapeDtypeStruct((m, n_heads * HEAD_DIM), BF16),
                   jax.ShapeDtypeStruct(w1.shape, BF16), jax.ShapeDtypeStruct(w2.shape, BF16)],
        scratch_shapes=[pltpu.VMEM((hb, HEAD_DIM, HEAD_DIM), F32)],
        compiler_params=pltpu.CompilerParams(
            dimension_semantics=("arbitrary", "arbitrary", "arbitrary"), vmem_limit_bytes=VMEM_LIMIT_BYTES),
    )(act, act, act, act, ab, gate_par, norm_w.reshape(1, HEAD_DIM), w1, w2)


def _hgrn_kernel(q_ref, k_ref, i_ref, g_ref, logf_ref, nw_ref, y_ref, st_scr, *, hb, tb):
    @pl.when(pl.program_id(2) == 0)
    def _():
        st_scr[...] = jnp.zeros_like(st_scr)

    row = lax.broadcasted_iota(jnp.int32, (CHUNK, HEAD_DIM), 0)
    lane = lax.broadcasted_iota(jnp.int32, (CHUNK, HEAD_DIM), 1)
    tri_twice = (row >= (lane & (CHUNK - 1))).astype(BF16)
    nw = nw_ref[...]

    n_chunks = tb // CHUNK
    items = [(j, c) for c in range(n_chunks) for j in range(hb)]
    ids = range(len(items))
    rs = [slice(c * CHUNK, (c + 1) * CHUNK) for _, c in items]
    hs = [slice(j * HEAD_DIM, (j + 1) * HEAD_DIM) for j, _ in items]
    keys = [k_ref[r, h].astype(F32) for r, h in zip(rs, hs)]
    qss = [q_ref[r, h].astype(F32) for r, h in zip(rs, hs)]
    vbs = [i_ref[r, h] for r, h in zip(rs, hs)]
    vs = [v.astype(F32) for v in vbs]
    bs = [_dot_exact_lhs(tri_twice, logf_ref[r, h]) for r, h in zip(rs, hs)]
    b_lasts = [b[CHUNK - 1:CHUNK, :] for b in bs]

    a_s = [jnp.zeros((CHUNK, HEAD_DIM), F32) for _ in ids]
    n = CHUNK // 2
    while n >= 1:
        upper = (row & n) != 0
        keep = ((row & -(2 * n)) == (lane & -(2 * n))) & upper & ((lane & n) == 0)
        offset = row & (2 * n - 1)
        for i in ids:
            b = bs[i]
            if 2 * n >= 8:
                b_ref = jnp.concatenate(
                    [jnp.broadcast_to(b[r0 + n - 1:r0 + n, :], (2 * n, HEAD_DIM)) for r0 in range(0, CHUNK, 2 * n)],
                    axis=0)
            else:
                b_ref = b
                for o in range(2 * n):
                    if o != n - 1:
                        b_ref = jnp.where(offset == o, pltpu.roll(b, (o - (n - 1)) % CHUNK, 0), b_ref)
            x = jnp.where(upper, qss[i], keys[i]) * jnp.exp(-jnp.abs(b - b_ref))
            a_s[i] = a_s[i] + jnp.where(keep, _dot_nt(x, _pad_rows(x)), 0.0)
        n //= 2

    intras = [_dot(a, _pad_rows(vb)) + jnp.sum(qs * key, axis=-1, keepdims=True) * v
              for a, vb, v, qs, key in zip(a_s, vbs, vs, qss, keys)]
    kvs = [_dot_tn(vb, key * jnp.exp(bl - b)) for vb, key, bl, b in zip(vbs, keys, b_lasts, bs)]
    qis = [qs * jnp.exp(b) for qs, b in zip(qss, bs)]

    states = [st_scr[j] for j in range(hb)]
    for i, (j, _) in enumerate(items):
        o = _dot_nt(qis[i], states[j]) + intras[i]
        states[j] = states[j] * jnp.exp(b_lasts[i]) + kvs[i]
        y_ref[rs[i], hs[i]] = _head_norm_gate(o, nw, g_ref[rs[i], hs[i]].astype(F32)).astype(y_ref.dtype)
    for j in range(hb):
        st_scr[j] = states[j]


def _hgrn(act, logf, norm_w, batch, seq, n_heads, hb, tb):
    m = act.shape[0]
    nt = seq // tb
    ng = n_heads // hb
    width = hb * HEAD_DIM
    tok = lambda part: pl.BlockSpec((tb, width), lambda b, g, t: (b * nt + t, part * ng + g))
    return pl.pallas_call(
        functools.partial(_hgrn_kernel, hb=hb, tb=tb),
        grid=(batch, ng, nt),
        in_specs=[
            tok(_HQ), tok(_HF), tok(_HI), tok(_HG),
            pl.BlockSpec((tb, width), lambda b, g, t: (b * nt + t, g)),
            pl.BlockSpec((1, HEAD_DIM), lambda b, g, t: (0, 0)),
        ],
        out_specs=pl.BlockSpec((tb, width), lambda b, g, t: (b * nt + t, g)),
        out_shape=jax.ShapeDtypeStruct((m, n_heads * HEAD_DIM), BF16),
        scratch_shapes=[pltpu.VMEM((hb, HEAD_DIM, HEAD_DIM), F32)],
        compiler_params=pltpu.CompilerParams(
            dimension_semantics=("arbitrary", "arbitrary", "arbitrary"), vmem_limit_bytes=VMEM_LIMIT_BYTES),
    )(act, act, act, act, logf, norm_w.reshape(1, HEAD_DIM))


def _outproj_kernel(ya_ref, yb_ref, wa_ref, wb_ref, x_ref, o_ref):
    acc = jnp.dot(ya_ref[...], wa_ref[...], preferred_element_type=F32)
    acc = acc + jnp.dot(yb_ref[...], wb_ref[...], preferred_element_type=F32)
    o_ref[...] = x_ref[...] + acc


def _outproj(ya, yb, w_a, w_b, x2, tm):
    m, d = x2.shape
    return pl.pallas_call(
        _outproj_kernel,
        grid=(m // tm,),
        in_specs=[
            pl.BlockSpec((tm, ya.shape[1]), lambda i: (i, 0)),
            pl.BlockSpec((tm, yb.shape[1]), lambda i: (i, 0)),
            pl.BlockSpec(w_a.shape, lambda i: (0, 0)),
            pl.BlockSpec(w_b.shape, lambda i: (0, 0)),
            pl.BlockSpec((tm, d), lambda i: (i, 0)),
        ],
        out_specs=pl.BlockSpec((tm, d), lambda i: (i, 0)),
        out_shape=jax.ShapeDtypeStruct((m, d), F32),
        compiler_params=pltpu.CompilerParams(
            dimension_semantics=("arbitrary",), vmem_limit_bytes=VMEM_LIMIT_BYTES),
    )(ya, yb, w_a, w_b, x2)


def _mlp_kernel(h_ref, nw_ref, w1_ref, w2_ref, fw_ref, o_ref, n_scr, *, final_norm):
    f = pl.program_id(1)

    @pl.when(f == 0)
    def _():
        x = h_ref[...]
        n = x * lax.rsqrt(jnp.mean(x * x, axis=-1, keepdims=True) + NORM_EPS) * nw_ref[...]
        n_scr[...] = n.astype(BF16)
        o_ref[...] = x

    hid = jnp.dot(n_scr[...], w1_ref[...], preferred_element_type=F32)
    hid = jnp.square(jnp.maximum(hid, 0.0)).astype(BF16)
    o_ref[...] += jnp.dot(hid, w2_ref[...], preferred_element_type=F32)

    if final_norm:
        @pl.when(f == pl.num_programs(1) - 1)
        def _():
            y = o_ref[...]
            o_ref[...] = y * lax.rsqrt(jnp.mean(y * y, axis=-1, keepdims=True) + NORM_EPS) * fw_ref[...]


def _mlp(h, norm_w, w1, w2, final_w, final_norm, tm, tf):
    m, d = h.shape
    ff = w1.shape[1]
    return pl.pallas_call(
        functools.partial(_mlp_kernel, final_norm=final_norm),
        grid=(m // tm, ff // tf),
        in_specs=[
            pl.BlockSpec((tm, d), lambda i, f: (i, 0)),
            pl.BlockSpec((1, d), lambda i, f: (0, 0)),
            pl.BlockSpec((d, tf), lambda i, f: (0, f)),
            pl.BlockSpec((tf, d), lambda i, f: (f, 0)),
            pl.BlockSpec((1, d), lambda i, f: (0, 0)),
        ],
        out_specs=pl.BlockSpec((tm, d), lambda i, f: (i, 0)),
        out_shape=jax.ShapeDtypeStruct((m, d), F32),
        scratch_shapes=[pltpu.VMEM((tm, d), BF16)],
        compiler_params=pltpu.CompilerParams(
            dimension_semantics=("arbitrary", "arbitrary"), vmem_limit_bytes=BIG_VMEM_LIMIT_BYTES),
    )(h, norm_w.reshape(1, d), w1, w2, final_w.reshape(1, d))


def _tile(n, target):
    t = min(n, target)
    while n % t:
        t //= 2
    return t


def kernel(x, w_in, conv_w, gdn_a_log, gdn_dt_bias, gdn_norm_w, hgrn_lb_logits, hgrn_norm_w, w_out,
           norm_mix_w, norm_ffn_w, w_ff1, w_ff2, norm_final_w):
    batch, seq, d_model = x.shape
    depth = w_in.shape[0]
    gh = gdn_a_log.shape[1]
    hh = hgrn_lb_logits.shape[1] // HEAD_DIM
    gw, hw = gh * HEAD_DIM, hh * HEAD_DIM
    assert 2 * gh <= HEAD_DIM and seq % CHUNK == 0
    assert gw == hw and w_in.shape[2] == 4 * gw + 2 * gh + 4 * hw

    m = batch * seq
    tm = _tile(m, ROW_TILE)
    gdn_hb, gdn_tb = _tile(gh, GDN_BLOCK[0]), _tile(seq, GDN_BLOCK[1])
    hgrn_hb, hgrn_tb = _tile(hh, HGRN_BLOCK[0]), _tile(seq, HGRN_BLOCK[1])
    h = x.reshape(m, d_model)
    for layer in range(depth):
        o1 = 4 * gw
        o2 = o1 + 2 * gh
        w_gdn = w_in[layer, :, :o1].astype(BF16)
        w_src, w_gdn = lax.optimization_barrier((w_in, w_gdn))
        w_hgrn = w_src[layer, :, o2:].astype(BF16)
        w_src, w_hgrn = lax.optimization_barrier((w_src, w_hgrn))
        w_ab = jnp.pad(w_src[layer, :, o1:o2], ((0, 0), (0, HEAD_DIM - 2 * gh))).astype(BF16)
        gate_par = jnp.zeros((SUBLANES, HEAD_DIM), F32)
        gate_par = gate_par.at[0, :gh].set(gdn_a_log[layer]).at[1, :gh].set(gdn_dt_bias[layer])
        w_o = w_out[layer].astype(BF16)

        act, logf, ab = _inproj(h, norm_mix_w[layer], w_gdn, w_hgrn, w_ab, conv_w[layer], hgrn_lb_logits, seq,
                                layer, _tile(seq, INPROJ_ROW_TILE), gw)
        y_a, w1, w2 = _gdn(act, ab, gate_par, gdn_norm_w[layer], w_ff1[layer], w_ff2[layer], batch, seq, gh,
                           gdn_hb, gdn_tb)
        y_b = _hgrn(act, logf, hgrn_norm_w[layer], batch, seq, hh, hgrn_hb, hgrn_tb)
        h = _outproj(y_a, y_b, w_o[:gw], w_o[gw:], h, tm)
        h = _mlp(h, norm_ffn_w[layer], w1, w2, norm_final_w, layer == depth - 1, _tile(m, MLP_TILE[0]),
                 _tile(w1.shape[1], MLP_TILE[1]))
    return h.reshape(batch, seq, d_model)
```

```python
import functools

import jax
import jax.numpy as jnp
from jax import lax
from jax.experimental import pallas as pl
from jax.experimental.pallas import tpu as pltpu

HEAD_DIM = 128
SUBLANES = 8
CHUNK = 64
CONV_WIDTH = 4
NORM_EPS = 1e-6
L2_EPS = 1e-6
VMEM_LIMIT_BYTES = 56 * 1024 * 1024

GDN_BLOCK = (8, 512)
HGRN_BLOCK = (8, 256)
ROW_TILE = 512
MLP_TILE = (1024, 1024)
BIG_VMEM_LIMIT_BYTES = 62 * 1024 * 1024
INPROJ_ROW_TILE = 1024
INPROJ_SUB = (1024, 256)

F32 = jnp.float32
BF16 = jnp.bfloat16


def _dot(a, b):
    return jnp.dot(a.astype(BF16), b.astype(BF16), preferred_element_type=F32)


def _dot_nt(a, b):
    return lax.dot_general(a.astype(BF16), b.astype(BF16), (((1,), (1,)), ((), ())),
                           preferred_element_type=F32)


def _dot_tn(a, b):
    return lax.dot_general(a.astype(BF16), b.astype(BF16), (((0,), (0,)), ((), ())),
                           preferred_element_type=F32)


def _dot_exact_lhs(a_twice, x):
    x1 = x.astype(BF16)
    x2 = (x - x1.astype(F32)).astype(BF16)
    return jnp.dot(a_twice, jnp.concatenate([x1, x2], axis=0), preferred_element_type=F32)


def _sigmoid(x):
    return 1.0 / (1.0 + jnp.exp(-x))


def _silu(x):
    return x * _sigmoid(x)


def _pad_rows(x):
    return jnp.concatenate([x, jnp.zeros_like(x)], axis=0)


def _head_norm_gate(o, nw, gate):
    return o * lax.rsqrt(jnp.mean(o * o, axis=-1, keepdims=True) + NORM_EPS) * nw * gate


_Q, _K, _V, _Z, _HQ, _HF, _HI, _HG = range(8)


def _inproj_kernel(x_ref, nw_ref, wg_ref, wh_ref, wab_ref, cw_ref, lb_ref, o_ref, logf_ref, oab_ref,
                   n_scr, halo_scr, *, tm, seq, layer):
    i = pl.program_id(0)
    j = pl.program_id(1)
    tn = o_ref.shape[1]

    @pl.when(j == 0)
    def _():
        x = x_ref[...]
        n = x * lax.rsqrt(jnp.mean(x * x, axis=-1, keepdims=True) + NORM_EPS) * nw_ref[...]
        nb = n.astype(BF16)
        n_scr[...] = nb
        oab_ref[...] = jnp.dot(nb, wab_ref[...], preferred_element_type=F32)

    sub_r, sub_c = min(INPROJ_SUB[0], tm), min(INPROJ_SUB[1], tn)
    blocks = [(slice(r, r + sub_r), slice(c, c + sub_c)) for r in range(0, tm, sub_r) for c in range(0, tn, sub_c)]

    def project(epilogue, w_ref):
        pending = None
        for rows, cols in blocks:
            acc = jnp.dot(n_scr[rows, :], w_ref[:, cols], preferred_element_type=F32)
            if pending is not None:
                epilogue(*pending)
            pending = (rows, cols, acc)
        epilogue(*pending)

    def conv_role(normalise):
        seq_start = (i * tm) % seq == 0
        halo_all = jnp.where(seq_start, 0.0, halo_scr[j])
        scale = jnp.where(j == _Q, HEAD_DIM ** -0.5, 1.0)
        halo_row = lax.broadcasted_iota(jnp.int32, (SUBLANES, sub_c), 0)
        tails = {}

        def epilogue(rows, cols, acc):
            w = cw_ref[:, cols]
            halo = halo_all[:, cols] if rows.start == 0 else tails[cols.start]
            tails[cols.start] = acc[sub_r - SUBLANES:sub_r]
            if rows.stop == tm:
                halo_scr[j, :, cols] = acc[sub_r - SUBLANES:sub_r]
            y = acc * w[CONV_WIDTH - 1:CONV_WIDTH, :]
            for s in range(1, CONV_WIDTH):
                rolled = pltpu.roll(acc, s, 0)
                head = jnp.where(halo_row < s, pltpu.roll(halo, s, 0), rolled[0:SUBLANES])
                shifted = jnp.concatenate([head, rolled[SUBLANES:]], axis=0)
                y = y + shifted * w[CONV_WIDTH - 1 - s:CONV_WIDTH - s, :]
            y = _silu(y)
            if not normalise:
                o_ref[rows, cols] = y.astype(BF16)
                return
            for h in range(sub_c // HEAD_DIM):
                yh = y[:, h * HEAD_DIM:(h + 1) * HEAD_DIM]
                inv_norm = lax.rsqrt(jnp.sum(yh * yh, axis=-1, keepdims=True) + L2_EPS)
                hs = slice(cols.start + h * HEAD_DIM, cols.start + (h + 1) * HEAD_DIM)
                o_ref[rows, hs] = (yh * (inv_norm * scale)).astype(BF16)

        project(epilogue, wg_ref)

    @pl.when(j < _V)
    def _():
        conv_role(True)

    @pl.when(j == _V)
    def _():
        conv_role(False)

    def silu_epilogue(rows, cols, acc):
        o_ref[rows, cols] = _silu(acc).astype(BF16)

    @pl.when(j == _Z)
    def _():
        project(silu_epilogue, wg_ref)

    @pl.when((j == _HQ) | (j == _HG))
    def _():
        project(silu_epilogue, wh_ref)

    @pl.when(j == _HI)
    def _():
        def epilogue(rows, cols, acc):
            o_ref[rows, cols] = acc.astype(BF16)
        project(epilogue, wh_ref)

    @pl.when(j == _HF)
    def _():
        logits = lb_ref[...]
        e = jnp.exp(logits - jnp.max(logits, axis=0, keepdims=True))
        lb_all = jnp.sum(e[:layer + 1], axis=0, keepdims=True) / jnp.sum(e, axis=0, keepdims=True)

        def epilogue(rows, cols, acc):
            lb = lb_all[:, cols]
            sig = _sigmoid(acc)
            o_ref[rows, cols] = ((1.0 - lb) * (1.0 - sig)).astype(BF16)
            logf_ref[rows, cols] = jnp.log(lb + (1.0 - lb) * sig)
        project(epilogue, wh_ref)


def _inproj(x2, norm_w, w_gdn, w_hgrn, w_ab, conv_w, lb_logits, seq, layer, tm, tn):
    m, d = x2.shape
    n = w_gdn.shape[1] + w_hgrn.shape[1]
    assert w_gdn.shape[1] == (_Z + 1) * tn and n == 8 * tn and seq % tm == 0
    return pl.pallas_call(
        functools.partial(_inproj_kernel, tm=tm, seq=seq, layer=layer),
        grid=(m // tm, n // tn),
        in_specs=[
            pl.BlockSpec((tm, d), lambda i, j: (i, 0)),
            pl.BlockSpec((1, d), lambda i, j: (0, 0)),
            pl.BlockSpec((d, tn), lambda i, j: (0, jnp.minimum(j, _Z))),
            pl.BlockSpec((d, tn), lambda i, j: (0, jnp.maximum(j, _HQ) - _HQ)),
            pl.BlockSpec((d, HEAD_DIM), lambda i, j: (0, 0)),
            pl.BlockSpec((CONV_WIDTH, tn), lambda i, j: (0, jnp.minimum(j, _V))),
            pl.BlockSpec((lb_logits.shape[0], tn), lambda i, j: (0, 0)),
        ],
        out_specs=[
            pl.BlockSpec((tm, tn), lambda i, j: (i, j)),
            pl.BlockSpec((tm, tn), lambda i, j: (i, 0)),
            pl.BlockSpec((tm, HEAD_DIM), lambda i, j: (i, 0)),
        ],
        out_shape=[jax.ShapeDtypeStruct((m, n), BF16), jax.ShapeDtypeStruct((m, tn), F32),
                   jax.ShapeDtypeStruct((m, HEAD_DIM), F32)],
        scratch_shapes=[pltpu.VMEM((tm, d), BF16), pltpu.VMEM((_V + 1, SUBLANES, tn), F32)],
        compiler_params=pltpu.CompilerParams(
            dimension_semantics=("arbitrary", "arbitrary"), vmem_limit_bytes=BIG_VMEM_LIMIT_BYTES),
    )(x2, norm_w.reshape(1, d), w_gdn, w_hgrn, w_ab, conv_w, lb_logits)


def _gdn_kernel(q_ref, k_ref, v_ref, z_ref, ab_ref, gp_ref, nw_ref, w1_ref, w2_ref, y_ref, w1b_ref, w2b_ref,
                s_scr, *, n_heads, hb, tb):
    h0 = pl.program_id(1) * hb
    w1b_ref[...] = w1_ref[...].astype(BF16)
    w2b_ref[...] = w2_ref[...].astype(BF16)

    @pl.when(pl.program_id(2) == 0)
    def _():
        s_scr[...] = jnp.zeros_like(s_scr)

    ab = ab_ref[...]
    gp = gp_ref[...]
    sp_in = ab + gp[1:2, :]
    g_tile = -jnp.exp(gp[0:1, :]) * (jnp.maximum(sp_in, 0.0) + jnp.log1p(jnp.exp(-jnp.abs(sp_in))))
    beta_tile = _sigmoid(ab)
    lane_t = lax.broadcasted_iota(jnp.int32, ab.shape, 1)

    def head_column(tile, idx):
        return jnp.broadcast_to(jnp.sum(jnp.where(lane_t == idx, tile, 0.0), axis=-1, keepdims=True), tile.shape)

    g_heads = [head_column(g_tile, h0 + j) for j in range(hb)]
    beta_heads = [head_column(beta_tile, h0 + j + n_heads) for j in range(hb)]

    row = lax.broadcasted_iota(jnp.int32, (CHUNK, HEAD_DIM), 0)
    lane = lax.broadcasted_iota(jnp.int32, (CHUNK, HEAD_DIM), 1)
    col = lane & (CHUNK - 1)
    left = lane < CHUNK
    incl = row >= col
    strict = row > col
    blk16 = (row >> 4) == (col >> 4)
    blk32 = (row >> 5) == (col >> 5)
    tri_twice = (row >= (lane & (CHUNK - 1))).astype(BF16)
    eye_left = jnp.where(lane == row, 1.0, 0.0)
    zeros = jnp.zeros((CHUNK, HEAD_DIM), F32)
    nw = nw_ref[...]

    n_chunks = tb // CHUNK
    items = [(j, c) for c in range(n_chunks) for j in range(hb)]
    rs = [slice(c * CHUNK, (c + 1) * CHUNK) for _, c in items]
    hs = [slice(j * HEAD_DIM, (j + 1) * HEAD_DIM) for j, _ in items]
    betas = [beta_heads[j][r] for (j, _), r in zip(items, rs)]
    gs = [g_heads[j][r] for (j, _), r in zip(items, rs)]
    dgs = [_dot_exact_lhs(tri_twice, jnp.concatenate([jnp.where(strict, g, 0.0), g], axis=1)) for g in gs]
    gcums = [dg[:, HEAD_DIM:] for dg in dgs]
    decays = [jnp.where(incl, jnp.exp(dg[:, :HEAD_DIM]), 0.0) for dg in dgs]
    g_lasts = [g[CHUNK - 1:CHUNK, :] for g in gcums]
    e_gs = [jnp.exp(g) for g in gcums]
    sds = [jnp.exp(gl) for gl in g_lasts]

    qbs = [q_ref[r, h] for r, h in zip(rs, hs)]
    kbs = [k_ref[r, h] for r, h in zip(rs, hs)]
    qns = [q.astype(F32) for q in qbs]
    kns = [k.astype(F32) for k in kbs]
    qks = [_dot_nt(jnp.concatenate([q, k], axis=0), jnp.concatenate([k, k], axis=0)) for q, k in zip(qbs, kbs)]
    attns = [jnp.where(left, qk[:CHUNK] * d, 0.0) for qk, d in zip(qks, decays)]
    lms = [jnp.where(strict & (~left), b * qk[CHUNK:] * d, 0.0) for b, qk, d in zip(betas, qks, decays)]

    ts = [eye_left - jnp.where(blk16, lm, 0.0) for lm in lms]
    for _ in range(4):
        ts = [_dot(t, jnp.concatenate([zeros, t], axis=0)) + jnp.where(left, t, 0.0) for t in ts]
    invs = [jnp.where(left, t, 0.0) for t in ts]
    for sel in (blk32 & (~blk16), ~blk32):
        w1s = [_dot(jnp.where(sel, lm, 0.0), jnp.concatenate([zeros, inv], axis=0)) for lm, inv in zip(lms, invs)]
        invs = [inv - _dot(inv, _pad_rows(w1)) for inv, w1 in zip(invs, w1s)]

    sols = [_dot(inv, _pad_rows(jnp.concatenate([b * e_g * kn, b * v_ref[r, h].astype(F32)], axis=1)))
            for inv, b, e_g, kn, r, h in zip(invs, betas, e_gs, kns, rs, hs)]
    k_ends = [kn * jnp.exp(gl - g) for kn, gl, g in zip(kns, g_lasts, gcums)]
    kws = [_dot_tn(k_end, sol) for k_end, sol in zip(k_ends, sols)]
    aws = [_dot(attn, _pad_rows(sol)) for attn, sol in zip(attns, sols)]
    lhss = [jnp.concatenate([kw[:, :HEAD_DIM], qn * e_g - aw[:, :HEAD_DIM]], axis=0)
            for kw, qn, e_g, aw in zip(kws, qns, e_gs, aws)]

    states = [s_scr[j] for j in range(hb)]
    for i, (j, _) in enumerate(items):
        r = _dot(lhss[i], states[j])
        o = r[HEAD_DIM:] + aws[i][:, HEAD_DIM:]
        states[j] = states[j] * sds[i] - r[:HEAD_DIM] + kws[i][:, HEAD_DIM:]
        y_ref[rs[i], hs[i]] = _head_norm_gate(o, nw, z_ref[rs[i], hs[i]].astype(F32)).astype(y_ref.dtype)
    for j in range(hb):
        s_scr[j] = states[j]


def _gdn(act, ab, gate_par, norm_w, w1, w2, batch, seq, n_heads, hb, tb):
    m = act.shape[0]
    nt = seq // tb
    ng = n_heads // hb
    width = hb * HEAD_DIM
    steps = batch * ng * nt
    r1, r2 = w1.shape[0] // steps, w2.shape[0] // steps
    assert r1 * steps == w1.shape[0] and r2 * steps == w2.shape[0] and r1 % 16 == 0 and r2 % 16 == 0
    tok = lambda part: pl.BlockSpec((tb, width), lambda b, g, t: (b * nt + t, part * ng + g))
    slab = lambda rows, cols: pl.BlockSpec((rows, cols), lambda b, g, t: ((b * ng + g) * nt + t, 0))
    return pl.pallas_call(
        functools.partial(_gdn_kernel, n_heads=n_heads, hb=hb, tb=tb),
        grid=(batch, ng, nt),
        in_specs=[
            tok(_Q), tok(_K), tok(_V), tok(_Z),
            pl.BlockSpec((tb, HEAD_DIM), lambda b, g, t: (b * nt + t, 0)),
            pl.BlockSpec((SUBLANES, HEAD_DIM), lambda b, g, t: (0, 0)),
            pl.BlockSpec((1, HEAD_DIM), lambda b, g, t: (0, 0)),
            slab(r1, w1.shape[1]), slab(r2, w2.shape[1]),
        ],
        out_specs=[pl.BlockSpec((tb, width), lambda b, g, t: (b * nt + t, g)),
                   slab(r1, w1.shape[1]), slab(r2, w2.shape[1])],
        out_shape=[jax.ShapeDtypeStruct((m, n_heads * HEAD_DIM), BF16),
                   jax.ShapeDtypeStruct(w1.shape, BF16), jax.ShapeDtypeStruct(w2.shape, BF16)],
        scratch_shapes=[pltpu.VMEM((hb, HEAD_DIM, HEAD_DIM), F32)],
        compiler_params=pltpu.CompilerParams(
            dimension_semantics=("arbitrary", "arbitrary", "arbitrary"), vmem_limit_bytes=VMEM_LIMIT_BYTES),
    )(act, act, act, act, ab, gate_par, norm_w.reshape(1, HEAD_DIM), w1, w2)


def _hgrn_kernel(q_ref, k_ref, i_ref, g_ref, logf_ref, nw_ref, y_ref, st_scr, *, hb, tb):
    @pl.when(pl.program_id(2) == 0)
    def _():
        st_scr[...] = jnp.zeros_like(st_scr)

    row = lax.broadcasted_iota(jnp.int32, (CHUNK, HEAD_DIM), 0)
    lane = lax.broadcasted_iota(jnp.int32, (CHUNK, HEAD_DIM), 1)
    tri_twice = (row >= (lane & (CHUNK - 1))).astype(BF16)
    nw = nw_ref[...]

    n_chunks = tb // CHUNK
    items = [(j, c) for c in range(n_chunks) for j in range(hb)]
    ids = range(len(items))
    rs = [slice(c * CHUNK, (c + 1) * CHUNK) for _, c in items]
    hs = [slice(j * HEAD_DIM, (j + 1) * HEAD_DIM) for j, _ in items]
    keys = [k_ref[r, h].astype(F32) for r, h in zip(rs, hs)]
    qss = [q_ref[r, h].astype(F32) for r, h in zip(rs, hs)]
    vbs = [i_ref[r, h] for r, h in zip(rs, hs)]
    vs = [v.astype(F32) for v in vbs]
    bs = [_dot_exact_lhs(tri_twice, logf_ref[r, h]) for r, h in zip(rs, hs)]
    b_lasts = [b[CHUNK - 1:CHUNK, :] for b in bs]

    a_s = [jnp.zeros((CHUNK, HEAD_DIM), F32) for _ in ids]
    n = CHUNK // 2
    while n >= 1:
        upper = (row & n) != 0
        keep = ((row & -(2 * n)) == (lane & -(2 * n))) & upper & ((lane & n) == 0)
        offset = row & (2 * n - 1)
        for i in ids:
            b = bs[i]
            if 2 * n >= 8:
                b_ref = jnp.concatenate(
                    [jnp.broadcast_to(b[r0 + n - 1:r0 + n, :], (2 * n, HEAD_DIM)) for r0 in range(0, CHUNK, 2 * n)],
                    axis=0)
            else:
                b_ref = b
                for o in range(2 * n):
                    if o != n - 1:
                        b_ref = jnp.where(offset == o, pltpu.roll(b, (o - (n - 1)) % CHUNK, 0), b_ref)
            x = jnp.where(upper, qss[i], keys[i]) * jnp.exp(-jnp.abs(b - b_ref))
            a_s[i] = a_s[i] + jnp.where(keep, _dot_nt(x, _pad_rows(x)), 0.0)
        n //= 2

    intras = [_dot(a, _pad_rows(vb)) + jnp.sum(qs * key, axis=-1, keepdims=True) * v
              for a, vb, v, qs, key in zip(a_s, vbs, vs, qss, keys)]
    kvs = [_dot_tn(vb, key * jnp.exp(bl - b)) for vb, key, bl, b in zip(vbs, keys, b_lasts, bs)]
    qis = [qs * jnp.exp(b) for qs, b in zip(qss, bs)]

    states = [st_scr[j] for j in range(hb)]
    for i, (j, _) in enumerate(items):
        o = _dot_nt(qis[i], states[j]) + intras[i]
        states[j] = states[j] * jnp.exp(b_lasts[i]) + kvs[i]
        y_ref[rs[i], hs[i]] = _head_norm_gate(o, nw, g_ref[rs[i], hs[i]].astype(F32)).astype(y_ref.dtype)
    for j in range(hb):
        st_scr[j] = states[j]


def _hgrn(act, logf, norm_w, batch, seq, n_heads, hb, tb):
    m = act.shape[0]
    nt = seq // tb
    ng = n_heads // hb
    width = hb * HEAD_DIM
    tok = lambda part: pl.BlockSpec((tb, width), lambda b, g, t: (b * nt + t, part * ng + g))
    return pl.pallas_call(
        functools.partial(_hgrn_kernel, hb=hb, tb=tb),
        grid=(batch, ng, nt),
        in_specs=[
            tok(_HQ), tok(_HF), tok(_HI), tok(_HG),
            pl.BlockSpec((tb, width), lambda b, g, t: (b * nt + t, g)),
            pl.BlockSpec((1, HEAD_DIM), lambda b, g, t: (0, 0)),
        ],
        out_specs=pl.BlockSpec((tb, width), lambda b, g, t: (b * nt + t, g)),
        out_shape=jax.ShapeDtypeStruct((m, n_heads * HEAD_DIM), BF16),
        scratch_shapes=[pltpu.VMEM((hb, HEAD_DIM, HEAD_DIM), F32)],
        compiler_params=pltpu.CompilerParams(
            dimension_semantics=("arbitrary", "arbitrary", "arbitrary"), vmem_limit_bytes=VMEM_LIMIT_BYTES),
    )(act, act, act, act, logf, norm_w.reshape(1, HEAD_DIM))


def _outproj_kernel(ya_ref, yb_ref, wa_ref, wb_ref, x_ref, o_ref):
    acc = jnp.dot(ya_ref[...], wa_ref[...], preferred_element_type=F32)
    acc = acc + jnp.dot(yb_ref[...], wb_ref[...], preferred_element_type=F32)
    o_ref[...] = x_ref[...] + acc


def _outproj(ya, yb, w_a, w_b, x2, tm):
    m, d = x2.shape
    return pl.pallas_call(
        _outproj_kernel,
        grid=(m // tm,),
        in_specs=[
            pl.BlockSpec((tm, ya.shape[1]), lambda i: (i, 0)),
            pl.BlockSpec((tm, yb.shape[1]), lambda i: (i, 0)),
            pl.BlockSpec(w_a.shape, lambda i: (0, 0)),
            pl.BlockSpec(w_b.shape, lambda i: (0, 0)),
            pl.BlockSpec((tm, d), lambda i: (i, 0)),
        ],
        out_specs=pl.BlockSpec((tm, d), lambda i: (i, 0)),
        out_shape=jax.ShapeDtypeStruct((m, d), F32),
        compiler_params=pltpu.CompilerParams(
            dimension_semantics=("arbitrary",), vmem_limit_bytes=VMEM_LIMIT_BYTES),
    )(ya, yb, w_a, w_b, x2)


def _mlp_kernel(h_ref, nw_ref, w1_ref, w2_ref, fw_ref, o_ref, n_scr, *, final_norm):
    f = pl.program_id(1)

    @pl.when(f == 0)
    def _():
        x = h_ref[...]
        n = x * lax.rsqrt(jnp.mean(x * x, axis=-1, keepdims=True) + NORM_EPS) * nw_ref[...]
        n_scr[...] = n.astype(BF16)
        o_ref[...] = x

    hid = jnp.dot(n_scr[...], w1_ref[...], preferred_element_type=F32)
    hid = jnp.square(jnp.maximum(hid, 0.0)).astype(BF16)
    o_ref[...] += jnp.dot(hid, w2_ref[...], preferred_element_type=F32)

    if final_norm:
        @pl.when(f == pl.num_programs(1) - 1)
        def _():
            y = o_ref[...]
            o_ref[...] = y * lax.rsqrt(jnp.mean(y * y, axis=-1, keepdims=True) + NORM_EPS) * fw_ref[...]


def _mlp(h, norm_w, w1, w2, final_w, final_norm, tm, tf):
    m, d = h.shape
    ff = w1.shape[1]
    return pl.pallas_call(
        functools.partial(_mlp_kernel, final_norm=final_norm),
        grid=(m // tm, ff // tf),
        in_specs=[
            pl.BlockSpec((tm, d), lambda i, f: (i, 0)),
            pl.BlockSpec((1, d), lambda i, f: (0, 0)),
            pl.BlockSpec((d, tf), lambda i, f: (0, f)),
            pl.BlockSpec((tf, d), lambda i, f: (f, 0)),
            pl.BlockSpec((1, d), lambda i, f: (0, 0)),
        ],
        out_specs=pl.BlockSpec((tm, d), lambda i, f: (i, 0)),
        out_shape=jax.ShapeDtypeStruct((m, d), F32),
        scratch_shapes=[pltpu.VMEM((tm, d), BF16)],
        compiler_params=pltpu.CompilerParams(
            dimension_semantics=("arbitrary", "arbitrary"), vmem_limit_bytes=BIG_VMEM_LIMIT_BYTES),
    )(h, norm_w.reshape(1, d), w1, w2, final_w.reshape(1, d))


def _tile(n, target):
    t = min(n, target)
    while n % t:
        t //= 2
    return t


def kernel(x, w_in, conv_w, gdn_a_log, gdn_dt_bias, gdn_norm_w, hgrn_lb_logits, hgrn_norm_w, w_out,
           norm_mix_w, norm_ffn_w, w_ff1, w_ff2, norm_final_w):
    batch, seq, d_model = x.shape
    depth = w_in.shape[0]
    gh = gdn_a_log.shape[1]
    hh = hgrn_lb_logits.shape[1] // HEAD_DIM
    gw, hw = gh * HEAD_DIM, hh * HEAD_DIM
    assert 2 * gh <= HEAD_DIM and seq % CHUNK == 0
    assert gw == hw and w_in.shape[2] == 4 * gw + 2 * gh + 4 * hw

    m = batch * seq
    tm = _tile(m, ROW_TILE)
    gdn_hb, gdn_tb = _tile(gh, GDN_BLOCK[0]), _tile(seq, GDN_BLOCK[1])
    hgrn_hb, hgrn_tb = _tile(hh, HGRN_BLOCK[0]), _tile(seq, HGRN_BLOCK[1])
    h = x.reshape(m, d_model)
    for layer in range(depth):
        wl = w_in[layer]
        o1 = 4 * gw
        o2 = o1 + 2 * gh
        w_gdn = wl[:, :o1].astype(BF16)
        wl, w_gdn = lax.optimization_barrier((wl, w_gdn))
        w_hgrn = wl[:, o2:].astype(BF16)
        w_ab = jnp.pad(wl[:, o1:o2], ((0, 0), (0, HEAD_DIM - 2 * gh))).astype(BF16)
        gate_par = jnp.zeros((SUBLANES, HEAD_DIM), F32)
        gate_par = gate_par.at[0, :gh].set(gdn_a_log[layer]).at[1, :gh].set(gdn_dt_bias[layer])
        w_o = w_out[layer].astype(BF16)

        act, logf, ab = _inproj(h, norm_mix_w[layer], w_gdn, w_hgrn, w_ab, conv_w[layer], hgrn_lb_logits, seq,
                                layer, _tile(seq, INPROJ_ROW_TILE), gw)
        y_a, w1, w2 = _gdn(act, ab, gate_par, gdn_norm_w[layer], w_ff1[layer], w_ff2[layer], batch, seq, gh,
                           gdn_hb, gdn_tb)
        y_b = _hgrn(act, logf, hgrn_norm_w[layer], batch, seq, hh, hgrn_hb, hgrn_tb)
        h = _outproj(y_a, y_b, w_o[:gw], w_o[gw:], h, tm)
        h = _mlp(h, norm_ffn_w[layer], w1, w2, norm_final_w, layer == depth - 1, _tile(m, MLP_TILE[0]),
                 _tile(w1.shape[1], MLP_TILE[1]))
    return h.reshape(batch, seq, d_model)
```

```python
import functools

import jax
import jax.numpy as jnp
from jax import lax
from jax.experimental import pallas as pl
from jax.experimental.pallas import tpu as pltpu

HEAD_DIM = 128
SUBLANES = 8
CHUNK = 64
CONV_WIDTH = 4
NORM_EPS = 1e-6
L2_EPS = 1e-6
VMEM_LIMIT_BYTES = 56 * 1024 * 1024

MIX_BLOCK = (8, 256)
ROW_TILE = 512
MLP_TILE = (1024, 1024)
BIG_VMEM_LIMIT_BYTES = 62 * 1024 * 1024
INPROJ_ROW_TILE = 1024
INPROJ_SUB = (1024, 256)

F32 = jnp.float32
BF16 = jnp.bfloat16


def _dot(a, b):
    return jnp.dot(a.astype(BF16), b.astype(BF16), preferred_element_type=F32)


def _dot_nt(a, b):
    return lax.dot_general(a.astype(BF16), b.astype(BF16), (((1,), (1,)), ((), ())),
                           preferred_element_type=F32)


def _dot_tn(a, b):
    return lax.dot_general(a.astype(BF16), b.astype(BF16), (((0,), (0,)), ((), ())),
                           preferred_element_type=F32)


def _dot_exact_lhs(a_twice, x):
    x1 = x.astype(BF16)
    x2 = (x - x1.astype(F32)).astype(BF16)
    return jnp.dot(a_twice, jnp.concatenate([x1, x2], axis=0), preferred_element_type=F32)


def _sigmoid(x):
    return 1.0 / (1.0 + jnp.exp(-x))


def _silu(x):
    return x * _sigmoid(x)


def _pad_rows(x):
    return jnp.concatenate([x, jnp.zeros_like(x)], axis=0)


def _head_norm_gate(o, nw, gate):
    return o * lax.rsqrt(jnp.mean(o * o, axis=-1, keepdims=True) + NORM_EPS) * nw * gate


_Q, _K, _V, _Z, _HQ, _HF, _HI, _HG = range(8)


def _inproj_kernel(x_ref, nw_ref, wg_ref, wh_ref, wab_ref, cw_ref, lb_ref, o_ref, logf_ref, oab_ref,
                   n_scr, halo_scr, *, tm, seq, layer):
    i = pl.program_id(0)
    j = pl.program_id(1)
    tn = o_ref.shape[1]

    @pl.when(j == 0)
    def _():
        x = x_ref[...]
        n = x * lax.rsqrt(jnp.mean(x * x, axis=-1, keepdims=True) + NORM_EPS) * nw_ref[...]
        nb = n.astype(BF16)
        n_scr[...] = nb
        oab_ref[...] = jnp.dot(nb, wab_ref[...], preferred_element_type=F32)

    sub_r, sub_c = min(INPROJ_SUB[0], tm), min(INPROJ_SUB[1], tn)
    blocks = [(slice(r, r + sub_r), slice(c, c + sub_c)) for r in range(0, tm, sub_r) for c in range(0, tn, sub_c)]

    def project(epilogue, w_ref):
        pending = None
        for rows, cols in blocks:
            acc = jnp.dot(n_scr[rows, :], w_ref[:, cols], preferred_element_type=F32)
            if pending is not None:
                epilogue(*pending)
            pending = (rows, cols, acc)
        epilogue(*pending)

    def conv_role(normalise):
        seq_start = (i * tm) % seq == 0
        halo_all = jnp.where(seq_start, 0.0, halo_scr[j])
        scale = jnp.where(j == _Q, HEAD_DIM ** -0.5, 1.0)
        halo_row = lax.broadcasted_iota(jnp.int32, (SUBLANES, sub_c), 0)
        tails = {}

        def epilogue(rows, cols, acc):
            w = cw_ref[:, cols]
            halo = halo_all[:, cols] if rows.start == 0 else tails[cols.start]
            tails[cols.start] = acc[sub_r - SUBLANES:sub_r]
            if rows.stop == tm:
                halo_scr[j, :, cols] = acc[sub_r - SUBLANES:sub_r]
            y = acc * w[CONV_WIDTH - 1:CONV_WIDTH, :]
            for s in range(1, CONV_WIDTH):
                rolled = pltpu.roll(acc, s, 0)
                head = jnp.where(halo_row < s, pltpu.roll(halo, s, 0), rolled[0:SUBLANES])
                shifted = jnp.concatenate([head, rolled[SUBLANES:]], axis=0)
                y = y + shifted * w[CONV_WIDTH - 1 - s:CONV_WIDTH - s, :]
            y = _silu(y)
            if not normalise:
                o_ref[rows, cols] = y.astype(BF16)
                return
            for h in range(sub_c // HEAD_DIM):
                yh = y[:, h * HEAD_DIM:(h + 1) * HEAD_DIM]
                inv_norm = lax.rsqrt(jnp.sum(yh * yh, axis=-1, keepdims=True) + L2_EPS)
                hs = slice(cols.start + h * HEAD_DIM, cols.start + (h + 1) * HEAD_DIM)
                o_ref[rows, hs] = (yh * (inv_norm * scale)).astype(BF16)

        project(epilogue, wg_ref)

    @pl.when(j < _V)
    def _():
        conv_role(True)

    @pl.when(j == _V)
    def _():
        conv_role(False)

    def silu_epilogue(rows, cols, acc):
        o_ref[rows, cols] = _silu(acc).astype(BF16)

    @pl.when(j == _Z)
    def _():
        project(silu_epilogue, wg_ref)

    @pl.when((j == _HQ) | (j == _HG))
    def _():
        project(silu_epilogue, wh_ref)

    @pl.when(j == _HI)
    def _():
        def epilogue(rows, cols, acc):
            o_ref[rows, cols] = acc.astype(BF16)
        project(epilogue, wh_ref)

    @pl.when(j == _HF)
    def _():
        logits = lb_ref[...]
        e = jnp.exp(logits - jnp.max(logits, axis=0, keepdims=True))
        lb_all = jnp.sum(e[:layer + 1], axis=0, keepdims=True) / jnp.sum(e, axis=0, keepdims=True)

        def epilogue(rows, cols, acc):
            lb = lb_all[:, cols]
            sig = _sigmoid(acc)
            o_ref[rows, cols] = ((1.0 - lb) * (1.0 - sig)).astype(BF16)
            logf_ref[rows, cols] = jnp.log(lb + (1.0 - lb) * sig)
        project(epilogue, wh_ref)


def _inproj(x2, norm_w, w_gdn, w_hgrn, w_ab, conv_w, lb_logits, seq, layer, tm, tn):
    m, d = x2.shape
    n = w_gdn.shape[1] + w_hgrn.shape[1]
    assert w_gdn.shape[1] == (_Z + 1) * tn and n == 8 * tn and seq % tm == 0
    return pl.pallas_call(
        functools.partial(_inproj_kernel, tm=tm, seq=seq, layer=layer),
        grid=(m // tm, n // tn),
        in_specs=[
            pl.BlockSpec((tm, d), lambda i, j: (i, 0)),
            pl.BlockSpec((1, d), lambda i, j: (0, 0)),
            pl.BlockSpec((d, tn), lambda i, j: (0, jnp.minimum(j, _Z))),
            pl.BlockSpec((d, tn), lambda i, j: (0, jnp.maximum(j, _HQ) - _HQ)),
            pl.BlockSpec((d, HEAD_DIM), lambda i, j: (0, 0)),
            pl.BlockSpec((CONV_WIDTH, tn), lambda i, j: (0, jnp.minimum(j, _V))),
            pl.BlockSpec((lb_logits.shape[0], tn), lambda i, j: (0, 0)),
        ],
        out_specs=[
            pl.BlockSpec((tm, tn), lambda i, j: (i, j)),
            pl.BlockSpec((tm, tn), lambda i, j: (i, 0)),
            pl.BlockSpec((tm, HEAD_DIM), lambda i, j: (i, 0)),
        ],
        out_shape=[jax.ShapeDtypeStruct((m, n), BF16), jax.ShapeDtypeStruct((m, tn), F32),
                   jax.ShapeDtypeStruct((m, HEAD_DIM), F32)],
        scratch_shapes=[pltpu.VMEM((tm, d), BF16), pltpu.VMEM((_V + 1, SUBLANES, tn), F32)],
        compiler_params=pltpu.CompilerParams(
            dimension_semantics=("arbitrary", "arbitrary"), vmem_limit_bytes=BIG_VMEM_LIMIT_BYTES),
    )(x2, norm_w.reshape(1, d), w_gdn, w_hgrn, w_ab, conv_w, lb_logits)


def _gdn_kernel(q_ref, k_ref, v_ref, z_ref, ab_ref, gp_ref, nw_ref, w1_ref, w2_ref, y_ref, w1b_ref, w2b_ref,
                s_scr, *, n_heads, hb, tb):
    h0 = pl.program_id(1) * hb
    w1b_ref[...] = w1_ref[...].astype(BF16)
    w2b_ref[...] = w2_ref[...].astype(BF16)

    ab = ab_ref[...]
    gp = gp_ref[...]
    sp_in = ab + gp[1:2, :]
    g_tile = -jnp.exp(gp[0:1, :]) * (jnp.maximum(sp_in, 0.0) + jnp.log1p(jnp.exp(-jnp.abs(sp_in))))
    beta_tile = _sigmoid(ab)
    lane_t = lax.broadcasted_iota(jnp.int32, ab.shape, 1)

    def head_column(tile, idx):
        return jnp.broadcast_to(jnp.sum(jnp.where(lane_t == idx, tile, 0.0), axis=-1, keepdims=True), tile.shape)

    g_heads = [head_column(g_tile, h0 + j) for j in range(hb)]
    beta_heads = [head_column(beta_tile, h0 + j + n_heads) for j in range(hb)]

    row = lax.broadcasted_iota(jnp.int32, (CHUNK, HEAD_DIM), 0)
    lane = lax.broadcasted_iota(jnp.int32, (CHUNK, HEAD_DIM), 1)
    col = lane & (CHUNK - 1)
    left = lane < CHUNK
    incl = row >= col
    strict = row > col
    blk16 = (row >> 4) == (col >> 4)
    blk32 = (row >> 5) == (col >> 5)
    tri_twice = (row >= (lane & (CHUNK - 1))).astype(BF16)
    eye_left = jnp.where(lane == row, 1.0, 0.0)
    zeros = jnp.zeros((CHUNK, HEAD_DIM), F32)
    nw = nw_ref[...]

    n_chunks = tb // CHUNK
    items = [(j, c) for c in range(n_chunks) for j in range(hb)]
    rs = [slice(c * CHUNK, (c + 1) * CHUNK) for _, c in items]
    hs = [slice(j * HEAD_DIM, (j + 1) * HEAD_DIM) for j, _ in items]
    betas = [beta_heads[j][r] for (j, _), r in zip(items, rs)]
    gs = [g_heads[j][r] for (j, _), r in zip(items, rs)]
    dgs = [_dot_exact_lhs(tri_twice, jnp.concatenate([jnp.where(strict, g, 0.0), g], axis=1)) for g in gs]
    gcums = [dg[:, HEAD_DIM:] for dg in dgs]
    decays = [jnp.where(incl, jnp.exp(dg[:, :HEAD_DIM]), 0.0) for dg in dgs]
    g_lasts = [g[CHUNK - 1:CHUNK, :] for g in gcums]
    e_gs = [jnp.exp(g) for g in gcums]
    sds = [jnp.exp(gl) for gl in g_lasts]

    qbs = [q_ref[r, h] for r, h in zip(rs, hs)]
    kbs = [k_ref[r, h] for r, h in zip(rs, hs)]
    qns = [q.astype(F32) for q in qbs]
    kns = [k.astype(F32) for k in kbs]
    qks = [_dot_nt(jnp.concatenate([q, k], axis=0), jnp.concatenate([k, k], axis=0)) for q, k in zip(qbs, kbs)]
    attns = [jnp.where(left, qk[:CHUNK] * d, 0.0) for qk, d in zip(qks, decays)]
    lms = [jnp.where(strict & (~left), b * qk[CHUNK:] * d, 0.0) for b, qk, d in zip(betas, qks, decays)]

    ts = [eye_left - jnp.where(blk16, lm, 0.0) for lm in lms]
    for _ in range(4):
        ts = [_dot(t, jnp.concatenate([zeros, t], axis=0)) + jnp.where(left, t, 0.0) for t in ts]
    invs = [jnp.where(left, t, 0.0) for t in ts]
    for sel in (blk32 & (~blk16), ~blk32):
        w1s = [_dot(jnp.where(sel, lm, 0.0), jnp.concatenate([zeros, inv], axis=0)) for lm, inv in zip(lms, invs)]
        invs = [inv - _dot(inv, _pad_rows(w1)) for inv, w1 in zip(invs, w1s)]

    sols = [_dot(inv, _pad_rows(jnp.concatenate([b * e_g * kn, b * v_ref[r, h].astype(F32)], axis=1)))
            for inv, b, e_g, kn, r, h in zip(invs, betas, e_gs, kns, rs, hs)]
    k_ends = [kn * jnp.exp(gl - g) for kn, gl, g in zip(kns, g_lasts, gcums)]
    kws = [_dot_tn(k_end, sol) for k_end, sol in zip(k_ends, sols)]
    aws = [_dot(attn, _pad_rows(sol)) for attn, sol in zip(attns, sols)]
    lhss = [jnp.concatenate([kw[:, :HEAD_DIM], qn * e_g - aw[:, :HEAD_DIM]], axis=0)
            for kw, qn, e_g, aw in zip(kws, qns, e_gs, aws)]

    states = [s_scr[j] for j in range(hb)]
    for i, (j, _) in enumerate(items):
        r = _dot(lhss[i], states[j])
        o = r[HEAD_DIM:] + aws[i][:, HEAD_DIM:]
        states[j] = states[j] * sds[i] - r[:HEAD_DIM] + kws[i][:, HEAD_DIM:]
        y_ref[rs[i], hs[i]] = _head_norm_gate(o, nw, z_ref[rs[i], hs[i]].astype(F32)).astype(y_ref.dtype)
    for j in range(hb):
        s_scr[j] = states[j]


def _hgrn_kernel(q_ref, k_ref, i_ref, g_ref, logf_ref, nw_ref, y_ref, st_scr, *, hb, tb):
    row = lax.broadcasted_iota(jnp.int32, (CHUNK, HEAD_DIM), 0)
    lane = lax.broadcasted_iota(jnp.int32, (CHUNK, HEAD_DIM), 1)
    tri_twice = (row >= (lane & (CHUNK - 1))).astype(BF16)
    nw = nw_ref[...]

    n_chunks = tb // CHUNK
    items = [(j, c) for c in range(n_chunks) for j in range(hb)]
    ids = range(len(items))
    rs = [slice(c * CHUNK, (c + 1) * CHUNK) for _, c in items]
    hs = [slice(j * HEAD_DIM, (j + 1) * HEAD_DIM) for j, _ in items]
    keys = [k_ref[r, h].astype(F32) for r, h in zip(rs, hs)]
    qss = [q_ref[r, h].astype(F32) for r, h in zip(rs, hs)]
    vbs = [i_ref[r, h] for r, h in zip(rs, hs)]
    vs = [v.astype(F32) for v in vbs]
    bs = [_dot_exact_lhs(tri_twice, logf_ref[r, h]) for r, h in zip(rs, hs)]
    b_lasts = [b[CHUNK - 1:CHUNK, :] for b in bs]

    a_s = [jnp.zeros((CHUNK, HEAD_DIM), F32) for _ in ids]
    n = CHUNK // 2
    while n >= 1:
        upper = (row & n) != 0
        keep = ((row & -(2 * n)) == (lane & -(2 * n))) & upper & ((lane & n) == 0)
        offset = row & (2 * n - 1)
        for i in ids:
            b = bs[i]
            if 2 * n >= 8:
                b_ref = jnp.concatenate(
                    [jnp.broadcast_to(b[r0 + n - 1:r0 + n, :], (2 * n, HEAD_DIM)) for r0 in range(0, CHUNK, 2 * n)],
                    axis=0)
            else:
                b_ref = b
                for o in range(2 * n):
                    if o != n - 1:
                        b_ref = jnp.where(offset == o, pltpu.roll(b, (o - (n - 1)) % CHUNK, 0), b_ref)
            x = jnp.where(upper, qss[i], keys[i]) * jnp.exp(-jnp.abs(b - b_ref))
            a_s[i] = a_s[i] + jnp.where(keep, _dot_nt(x, _pad_rows(x)), 0.0)
        n //= 2

    intras = [_dot(a, _pad_rows(vb)) + jnp.sum(qs * key, axis=-1, keepdims=True) * v
              for a, vb, v, qs, key in zip(a_s, vbs, vs, qss, keys)]
    kvs = [_dot_tn(vb, key * jnp.exp(bl - b)) for vb, key, bl, b in zip(vbs, keys, b_lasts, bs)]
    qis = [qs * jnp.exp(b) for qs, b in zip(qss, bs)]

    states = [st_scr[j] for j in range(hb)]
    for i, (j, _) in enumerate(items):
        o = _dot_nt(qis[i], states[j]) + intras[i]
        states[j] = states[j] * jnp.exp(b_lasts[i]) + kvs[i]
        y_ref[rs[i], hs[i]] = _head_norm_gate(o, nw, g_ref[rs[i], hs[i]].astype(F32)).astype(y_ref.dtype)
    for j in range(hb):
        st_scr[j] = states[j]


def _mixers_kernel(q_ref, k_ref, v_ref, z_ref, ab_ref, gp_ref, gnw_ref, w1_ref, w2_ref,
                   hq_ref, hk_ref, hi_ref, hg_ref, logf_ref, hnw_ref,
                   ya_ref, w1b_ref, w2b_ref, yb_ref, s_scr, st_scr, *, n_heads, hb, tb):
    @pl.when(pl.program_id(2) == 0)
    def _():
        s_scr[...] = jnp.zeros_like(s_scr)
        st_scr[...] = jnp.zeros_like(st_scr)

    _gdn_kernel(q_ref, k_ref, v_ref, z_ref, ab_ref, gp_ref, gnw_ref, w1_ref, w2_ref, ya_ref, w1b_ref, w2b_ref,
                s_scr, n_heads=n_heads, hb=hb, tb=tb)
    _hgrn_kernel(hq_ref, hk_ref, hi_ref, hg_ref, logf_ref, hnw_ref, yb_ref, st_scr, hb=hb, tb=tb)


def _mixers(act, logf, ab, gate_par, gdn_norm_w, hgrn_norm_w, w1, w2, batch, seq, n_heads, hb, tb):
    m = act.shape[0]
    nt = seq // tb
    ng = n_heads // hb
    width = hb * HEAD_DIM
    steps = batch * ng * nt
    r1, r2 = w1.shape[0] // steps, w2.shape[0] // steps
    assert r1 * steps == w1.shape[0] and r2 * steps == w2.shape[0] and r1 % 16 == 0 and r2 % 16 == 0
    tok = lambda part: pl.BlockSpec((tb, width), lambda b, g, t: (b * nt + t, part * ng + g))
    slab = lambda rows, cols: pl.BlockSpec((rows, cols), lambda b, g, t: ((b * ng + g) * nt + t, 0))
    const = lambda rows: pl.BlockSpec((rows, HEAD_DIM), lambda b, g, t: (0, 0))
    y_spec = pl.BlockSpec((tb, width), lambda b, g, t: (b * nt + t, g))
    y_shape = jax.ShapeDtypeStruct((m, n_heads * HEAD_DIM), BF16)
    state = pltpu.VMEM((hb, HEAD_DIM, HEAD_DIM), F32)
    return pl.pallas_call(
        functools.partial(_mixers_kernel, n_heads=n_heads, hb=hb, tb=tb),
        grid=(batch, ng, nt),
        in_specs=[
            tok(_Q), tok(_K), tok(_V), tok(_Z),
            pl.BlockSpec((tb, HEAD_DIM), lambda b, g, t: (b * nt + t, 0)),
            const(SUBLANES), const(1), slab(r1, w1.shape[1]), slab(r2, w2.shape[1]),
            tok(_HQ), tok(_HF), tok(_HI), tok(_HG), y_spec, const(1),
        ],
        out_specs=[y_spec, slab(r1, w1.shape[1]), slab(r2, w2.shape[1]), y_spec],
        out_shape=[y_shape, jax.ShapeDtypeStruct(w1.shape, BF16), jax.ShapeDtypeStruct(w2.shape, BF16), y_shape],
        scratch_shapes=[state, state],
        compiler_params=pltpu.CompilerParams(
            dimension_semantics=("arbitrary", "arbitrary", "arbitrary"), vmem_limit_bytes=VMEM_LIMIT_BYTES),
    )(act, act, act, act, ab, gate_par, gdn_norm_w.reshape(1, HEAD_DIM), w1, w2,
      act, act, act, act, logf, hgrn_norm_w.reshape(1, HEAD_DIM))


def _outproj_kernel(ya_ref, yb_ref, wa_ref, wb_ref, x_ref, o_ref):
    acc = jnp.dot(ya_ref[...], wa_ref[...], preferred_element_type=F32)
    acc = acc + jnp.dot(yb_ref[...], wb_ref[...], preferred_element_type=F32)
    o_ref[...] = x_ref[...] + acc


def _outproj(ya, yb, w_a, w_b, x2, tm):
    m, d = x2.shape
    return pl.pallas_call(
        _outproj_kernel,
        grid=(m // tm,),
        in_specs=[
            pl.BlockSpec((tm, ya.shape[1]), lambda i: (i, 0)),
            pl.BlockSpec((tm, yb.shape[1]), lambda i: (i, 0)),
            pl.BlockSpec(w_a.shape, lambda i: (0, 0)),
            pl.BlockSpec(w_b.shape, lambda i: (0, 0)),
            pl.BlockSpec((tm, d), lambda i: (i, 0)),
        ],
        out_specs=pl.BlockSpec((tm, d), lambda i: (i, 0)),
        out_shape=jax.ShapeDtypeStruct((m, d), F32),
        compiler_params=pltpu.CompilerParams(
            dimension_semantics=("arbitrary",), vmem_limit_bytes=VMEM_LIMIT_BYTES),
    )(ya, yb, w_a, w_b, x2)


def _mlp_kernel(h_ref, nw_ref, w1_ref, w2_ref, fw_ref, o_ref, n_scr, *, final_norm):
    f = pl.program_id(1)

    @pl.when(f == 0)
    def _():
        x = h_ref[...]
        n = x * lax.rsqrt(jnp.mean(x * x, axis=-1, keepdims=True) + NORM_EPS) * nw_ref[...]
        n_scr[...] = n.astype(BF16)
        o_ref[...] = x

    hid = jnp.dot(n_scr[...], w1_ref[...], preferred_element_type=F32)
    hid = jnp.square(jnp.maximum(hid, 0.0)).astype(BF16)
    o_ref[...] += jnp.dot(hid, w2_ref[...], preferred_element_type=F32)

    if final_norm:
        @pl.when(f == pl.num_programs(1) - 1)
        def _():
            y = o_ref[...]
            o_ref[...] = y * lax.rsqrt(jnp.mean(y * y, axis=-1, keepdims=True) + NORM_EPS) * fw_ref[...]


def _mlp(h, norm_w, w1, w2, final_w, final_norm, tm, tf):
    m, d = h.shape
    ff = w1.shape[1]
    return pl.pallas_call(
        functools.partial(_mlp_kernel, final_norm=final_norm),
        grid=(m // tm, ff // tf),
        in_specs=[
            pl.BlockSpec((tm, d), lambda i, f: (i, 0)),
            pl.BlockSpec((1, d), lambda i, f: (0, 0)),
            pl.BlockSpec((d, tf), lambda i, f: (0, f)),
            pl.BlockSpec((tf, d), lambda i, f: (f, 0)),
            pl.BlockSpec((1, d), lambda i, f: (0, 0)),
        ],
        out_specs=pl.BlockSpec((tm, d), lambda i, f: (i, 0)),
        out_shape=jax.ShapeDtypeStruct((m, d), F32),
        scratch_shapes=[pltpu.VMEM((tm, d), BF16)],
        compiler_params=pltpu.CompilerParams(
            dimension_semantics=("arbitrary", "arbitrary"), vmem_limit_bytes=BIG_VMEM_LIMIT_BYTES),
    )(h, norm_w.reshape(1, d), w1, w2, final_w.reshape(1, d))


def _tile(n, target):
    t = min(n, target)
    while n % t:
        t //= 2
    return t


def kernel(x, w_in, conv_w, gdn_a_log, gdn_dt_bias, gdn_norm_w, hgrn_lb_logits, hgrn_norm_w, w_out,
           norm_mix_w, norm_ffn_w, w_ff1, w_ff2, norm_final_w):
    batch, seq, d_model = x.shape
    depth = w_in.shape[0]
    gh = gdn_a_log.shape[1]
    hh = hgrn_lb_logits.shape[1] // HEAD_DIM
    gw, hw = gh * HEAD_DIM, hh * HEAD_DIM
    assert 2 * gh <= HEAD_DIM and seq % CHUNK == 0
    assert gw == hw and w_in.shape[2] == 4 * gw + 2 * gh + 4 * hw

    m = batch * seq
    tm = _tile(m, ROW_TILE)
    mix_hb, mix_tb = _tile(gh, MIX_BLOCK[0]), _tile(seq, MIX_BLOCK[1])
    h = x.reshape(m, d_model)
    for layer in range(depth):
        wl = w_in[layer]
        o1 = 4 * gw
        o2 = o1 + 2 * gh
        w_gdn = wl[:, :o1].astype(BF16)
        wl, w_gdn = lax.optimization_barrier((wl, w_gdn))
        w_hgrn = wl[:, o2:].astype(BF16)
        w_ab = jnp.pad(wl[:, o1:o2], ((0, 0), (0, HEAD_DIM - 2 * gh))).astype(BF16)
        gate_par = jnp.zeros((SUBLANES, HEAD_DIM), F32)
        gate_par = gate_par.at[0, :gh].set(gdn_a_log[layer]).at[1, :gh].set(gdn_dt_bias[layer])
        w_o = w_out[layer].astype(BF16)

        act, logf, ab = _inproj(h, norm_mix_w[layer], w_gdn, w_hgrn, w_ab, conv_w[layer], hgrn_lb_logits, seq,
                                layer, _tile(seq, INPROJ_ROW_TILE), gw)
        y_a, w1, w2, y_b = _mixers(act, logf, ab, gate_par, gdn_norm_w[layer], hgrn_norm_w[layer], w_ff1[layer],
                                   w_ff2[layer], batch, seq, gh, mix_hb, mix_tb)
        h = _outproj(y_a, y_b, w_o[:gw], w_o[gw:], h, tm)
        h = _mlp(h, norm_ffn_w[layer], w1, w2, norm_final_w, layer == depth - 1, _tile(m, MLP_TILE[0]),
                 _tile(w1.shape[1], MLP_TILE[1]))
    return h.reshape(batch, seq, d_model)
```
